```python
import jax, jax.numpy as jnp
from jax import lax
import numpy as np

D_MODEL = 1024
BATCH = 4
SEQ = 8192
DEPTH = 2

SSD_HEADS = 16
SSD_HEAD_DIM = 64
SSD_INNER = SSD_HEADS * SSD_HEAD_DIM
SSD_GROUPS = 2
SSD_STATE = 64
SSD_CONV = 4
SSD_CHUNK = 128
SSD_XBC = SSD_INNER + 2 * SSD_GROUPS * SSD_STATE
CONF_WIDTH = 512
CONF_KERNEL = 31
SC_WIDTH = 512
SC_KERNEL = 3
N_BRANCH = 3
D_FF = 2816
FFN_KERNEL = 3
EPS = 1e-6

OFF_Z = SSD_INNER
OFF_XBC = OFF_Z + SSD_XBC
OFF_DT = OFF_XBC + SSD_HEADS
OFF_CONF = OFF_DT + 2 * CONF_WIDTH
OFF_SC = OFF_CONF + 3 * SC_WIDTH
N_IN = OFF_SC + N_BRANCH * D_MODEL
IN_SPLITS = (OFF_Z, OFF_XBC, OFF_DT, OFF_CONF, OFF_SC)

kernel_name = "hybrid_ssd_conformer_shortconv_adaln"


def rms_norm(x, g):
    xf = x.astype(jnp.float32)
    y = xf * lax.rsqrt(jnp.mean(xf * xf, axis=-1, keepdims=True) + EPS)
    return (y * g).astype(x.dtype)


def layer_norm(x, g, b):
    xf = x.astype(jnp.float32)
    mu = jnp.mean(xf, axis=-1, keepdims=True)
    xc = xf - mu
    y = xc * lax.rsqrt(jnp.mean(xc * xc, axis=-1, keepdims=True) + EPS)
    return (y * g + b).astype(x.dtype)


def gated_group_rmsnorm(y, z, g):
    v = (y * jax.nn.silu(z)).astype(jnp.float32)
    v = v.reshape(*v.shape[:-1], SSD_GROUPS, -1)
    v = v * lax.rsqrt(jnp.mean(v * v, axis=-1, keepdims=True) + EPS)
    return (v.reshape(y.shape) * g).astype(z.dtype)


def causal_dwconv(x, w, b=None):
    k = w.shape[0]
    y = lax.conv_general_dilated(
        x, w[:, None, :].astype(x.dtype), window_strides=(1,), padding=[(k - 1, 0)],
        dimension_numbers=('NWC', 'WIO', 'NWC'), feature_group_count=x.shape[-1])
    return y if b is None else y + b


def adaln(c, w, b):
    mod = jax.nn.silu(c) @ w + b
    shift, scale, gate = jnp.split(mod[:, None, :], 3, axis=-1)
    return shift, scale, gate


def ssd_chunked(x, dt, a, b_mat, c_mat):
    bsz, l, h, p = x.shape
    g, n = b_mat.shape[-2:]
    r = h // g
    q = SSD_CHUNK
    nc = l // q
    xc = (x * dt[..., None]).reshape(bsz, nc, q, g, r, p)
    a_dt = (dt * a).astype(jnp.float32).reshape(bsz, nc, q, g, r)
    a_cum = jnp.cumsum(jnp.moveaxis(a_dt, 2, -1), axis=-1)
    bc = b_mat.reshape(bsz, nc, q, g, n)
    cc = c_mat.reshape(bsz, nc, q, g, n)
    causal = jnp.tril(jnp.ones((q, q), dtype=bool))
    seg = a_cum[..., :, None] - a_cum[..., None, :]
    decay = jnp.exp(jnp.where(causal, seg, -jnp.inf))
    cb = jnp.einsum('bclgn,bcsgn->bcgls', cc, bc)
    wts = cb[:, :, :, None] * decay
    y_diag = jnp.einsum('bcgrls,bcsgrp->bclgrp', wts, xc)
    decay_states = jnp.exp(a_cum[..., -1:] - a_cum)
    xd = xc * jnp.moveaxis(decay_states, -1, 2)[..., None]
    states = jnp.einsum('bcsgn,bcsgrp->bcgrpn', bc, xd)
    chunk_decay = jnp.exp(a_cum[..., -1])

    def step(hstate, inp):
        s, d = inp
        return hstate * d[..., None, None] + s, hstate

    init = jnp.zeros((bsz, g, r, p, n), dtype=states.dtype)
    _, prev = lax.scan(step, init, (jnp.moveaxis(states, 1, 0), jnp.moveaxis(chunk_decay, 1, 0)))
    prev = jnp.moveaxis(prev, 0, 1)
    decay_out = jnp.exp(jnp.moveaxis(a_cum, -1, 2))[..., None]
    y_off = jnp.einsum('bclgn,bcgrpn->bclgrp', cc, prev) * decay_out
    return (y_diag + y_off).reshape(bsz, l, h, p)


def token_mixers(h, w_in, b_gate, ssd_conv_w, ssd_conv_b, ssd_dt_bias, ssd_a_log, ssd_d,
                 ssd_norm_g, w_ssd_out, conf_conv_w, conf_conv_b, conf_ln_g, conf_ln_b,
                 w_conf_out, sc_conv_w, w_sc_out, w_o):
    bsz, l, _ = h.shape
    proj = h @ w_in
    z, xbc, dt, conf_in, sc_in, gates = jnp.split(proj, IN_SPLITS, axis=-1)
    xbc = jax.nn.silu(causal_dwconv(xbc, ssd_conv_w, ssd_conv_b))
    xs, bm, cm = jnp.split(xbc, [SSD_INNER, SSD_INNER + SSD_GROUPS * SSD_STATE], axis=-1)
    dt = jax.nn.softplus((dt + ssd_dt_bias).astype(jnp.float32))
    a = -jnp.exp(ssd_a_log.astype(jnp.float32))
    xs = xs.reshape(bsz, l, SSD_HEADS, SSD_HEAD_DIM)
    y = ssd_chunked(xs, dt, a,
                    bm.reshape(bsz, l, SSD_GROUPS, SSD_STATE),
                    cm.reshape(bsz, l, SSD_GROUPS, SSD_STATE))
    y = (y + xs * ssd_d[:, None]).reshape(bsz, l, SSD_INNER)
    y_a = gated_group_rmsnorm(y, z, ssd_norm_g) @ w_ssd_out
    u_val, u_gate = jnp.split(conf_in, 2, axis=-1)
    u = u_val * jax.nn.sigmoid(u_gate)
    u = causal_dwconv(u, conf_conv_w, conf_conv_b)
    u = layer_norm(u, conf_ln_g, conf_ln_b)
    y_b = jax.nn.silu(u) @ w_conf_out
    gb, gc, xv = jnp.split(sc_in, 3, axis=-1)
    y_c = (gb * causal_dwconv(gc * xv, sc_conv_w)) @ w_sc_out
    g_a, g_b, g_c = jnp.split(jax.nn.sigmoid(gates + b_gate), 3, axis=-1)
    merged = g_a * y_a + g_b * y_b + g_c * y_c
    return (merged @ w_o).astype(h.dtype)


def conv_ffn(h, w_up, conv_w, conv_b, w_down):
    u = causal_dwconv(h @ w_up, conv_w, conv_b)
    gate, val = jnp.split(u, 2, axis=-1)
    return (jax.nn.silu(gate) * val) @ w_down


def setup_inputs(seed: int = 0) -> dict:
    key = jax.random.key(seed)
    ks = iter(jax.random.split(key, 40))

    def nrm(shape, scale):
        return jax.random.normal(next(ks), shape, jnp.float32) * scale

    def gain(shape):
        return 1.0 + nrm(shape, 0.02)

    dt0 = jnp.exp(jax.random.uniform(next(ks), (DEPTH, SSD_HEADS), jnp.float32,
                                     np.float32(np.log(1e-3)), np.float32(np.log(1e-1))))
    return {
        "x": nrm((BATCH, SEQ, D_MODEL), 1.0),
        "c": nrm((BATCH, D_MODEL), 1.0),
        "ada_mix_w": nrm((DEPTH, D_MODEL, 3 * D_MODEL), D_MODEL ** -0.5),
        "ada_mix_b": nrm((DEPTH, 3 * D_MODEL), 0.02),
        "norm_mix_g": gain((DEPTH, D_MODEL)),
        "w_in": nrm((DEPTH, D_MODEL, N_IN), D_MODEL ** -0.5),
        "b_gate": nrm((DEPTH, N_BRANCH * D_MODEL), 0.02),
        "ssd_conv_w": nrm((DEPTH, SSD_CONV, SSD_XBC), SSD_CONV ** -0.5),
        "ssd_conv_b": nrm((DEPTH, SSD_XBC), 0.02),
        "ssd_dt_bias": dt0 + jnp.log(-jnp.expm1(-dt0)),
        "ssd_a_log": jnp.log(jax.random.uniform(next(ks), (DEPTH, SSD_HEADS), jnp.float32, 1.0, 16.0)),
        "ssd_d": gain((DEPTH, SSD_HEADS)),
        "ssd_norm_g": gain((DEPTH, SSD_INNER)),
        "w_ssd_out": nrm((DEPTH, SSD_INNER, D_MODEL), SSD_INNER ** -0.5),
        "conf_conv_w": nrm((DEPTH, CONF_KERNEL, CONF_WIDTH), CONF_KERNEL ** -0.5),
        "conf_conv_b": nrm((DEPTH, CONF_WIDTH), 0.02),
        "conf_ln_g": gain((DEPTH, CONF_WIDTH)),
        "conf_ln_b": nrm((DEPTH, CONF_WIDTH), 0.02),
        "w_conf_out": nrm((DEPTH, CONF_WIDTH, D_MODEL), CONF_WIDTH ** -0.5),
        "sc_conv_w": nrm((DEPTH, SC_KERNEL, SC_WIDTH), SC_KERNEL ** -0.5),
        "w_sc_out": nrm((DEPTH, SC_WIDTH, D_MODEL), SC_WIDTH ** -0.5),
        "w_o": nrm((DEPTH, D_MODEL, D_MODEL), D_MODEL ** -0.5),
        "ada_ffn_w": nrm((DEPTH, D_MODEL, 3 * D_MODEL), D_MODEL ** -0.5),
        "ada_ffn_b": nrm((DEPTH, 3 * D_MODEL), 0.02),
        "norm_ffn_g": gain((DEPTH, D_MODEL)),
        "w_up": nrm((DEPTH, D_MODEL, 2 * D_FF), D_MODEL ** -0.5),
        "ffn_conv_w": nrm((DEPTH, FFN_KERNEL, 2 * D_FF), FFN_KERNEL ** -0.5),
        "ffn_conv_b": nrm((DEPTH, 2 * D_FF), 0.02),
        "w_down": nrm((DEPTH, D_FF, D_MODEL), D_FF ** -0.5),
        "final_norm_g": gain((D_MODEL,)),
    }


def reference(x, c, ada_mix_w, ada_mix_b, norm_mix_g, w_in, b_gate, ssd_conv_w, ssd_conv_b,
              ssd_dt_bias, ssd_a_log, ssd_d, ssd_norm_g, w_ssd_out, conf_conv_w, conf_conv_b,
              conf_ln_g, conf_ln_b, w_conf_out, sc_conv_w, w_sc_out, w_o, ada_ffn_w, ada_ffn_b,
              norm_ffn_g, w_up, ffn_conv_w, ffn_conv_b, w_down, final_norm_g):
    for i in range(DEPTH):
        shift, scale, gate = adaln(c, ada_mix_w[i], ada_mix_b[i])
        h = rms_norm(x, norm_mix_g[i]) * (1 + scale) + shift
        mix = token_mixers(h, w_in[i], b_gate[i], ssd_conv_w[i], ssd_conv_b[i], ssd_dt_bias[i],
                           ssd_a_log[i], ssd_d[i], ssd_norm_g[i], w_ssd_out[i], conf_conv_w[i],
                           conf_conv_b[i], conf_ln_g[i], conf_ln_b[i], w_conf_out[i],
                           sc_conv_w[i], w_sc_out[i], w_o[i])
        x = x + (gate * mix).astype(x.dtype)
        shift, scale, gate = adaln(c, ada_ffn_w[i], ada_ffn_b[i])
        h = rms_norm(x, norm_ffn_g[i]) * (1 + scale) + shift
        x = x + (gate * conv_ffn(h, w_up[i], ffn_conv_w[i], ffn_conv_b[i], w_down[i])).astype(x.dtype)
    return rms_norm(x, final_norm_g)
```

```python
import functools

import numpy as np
import jax
import jax.numpy as jnp
from jax import lax
from jax.experimental import pallas as pl
from jax.experimental.pallas import tpu as pltpu

D_MODEL = 1024
DEPTH = 2
SSD_HEADS = 16
SSD_HEAD_DIM = 64
SSD_INNER = SSD_HEADS * SSD_HEAD_DIM
SSD_GROUPS = 2
SSD_STATE = 64
SSD_CONV = 4
SSD_CHUNK = 128
SSD_XBC = SSD_INNER + 2 * SSD_GROUPS * SSD_STATE
CONF_WIDTH = 512
CONF_KERNEL = 31
SC_WIDTH = 512
SC_KERNEL = 3
N_BRANCH = 3
D_FF = 2816
FFN_KERNEL = 3
EPS = 1e-6

LANES = 128
SUBLANES = 8
VMEM_LIMIT = 58 * 1024 * 1024

COL_Z = 0
COL_XBC = COL_Z + SSD_INNER
COL_DT = COL_XBC + SSD_XBC
COL_CONF = COL_DT + LANES
COL_SC = COL_CONF + 2 * CONF_WIDTH
COL_GATE = COL_SC + 3 * SC_WIDTH
N_IN_PAD = COL_GATE + N_BRANCH * D_MODEL

SEQ_TILE = 256
ROWS = 32
FFN_BLOCK = 256
CONF_HALO = 32
SMALL_HALO = SUBLANES

F32 = jnp.float32
BF16 = jnp.bfloat16


def _sigmoid(x):
    return 1.0 / (1.0 + jnp.exp(-x))


def _silu(x):
    return x * _sigmoid(x)


def _softplus(x):
    return jnp.maximum(x, 0.0) + jnp.log1p(jnp.exp(-jnp.abs(x)))


def _split_bf16(x, n):
    parts = []
    r = x
    for _ in range(n):
        p = r.astype(BF16)
        parts.append(p)
        r = r - p.astype(F32)
    return parts


def _dot(a, b):
    return jnp.dot(a, b, preferred_element_type=F32)


def _dot_split_lhs(x, w_bf16, n):
    acc = None
    for p in _split_bf16(x, n):
        t = _dot(p, w_bf16)
        acc = t if acc is None else acc + t
    return acc


def _dot_split_rhs(w_bf16, x, n):
    acc = None
    for p in _split_bf16(x, n):
        t = _dot(w_bf16, p)
        acc = t if acc is None else acc + t
    return acc


def _modulated_norm(x_ref, mod_ref, g_ref, h_ref, tile):
    g = g_ref[...]
    shift = mod_ref[0:1, :]
    scale1 = 1.0 + mod_ref[1:2, :]
    for r0 in range(0, tile, ROWS):
        x = x_ref[r0:r0 + ROWS, :]
        ms = jnp.mean(x * x, axis=-1, keepdims=True)
        y = x * lax.rsqrt(ms + EPS) * g
        h_ref[r0:r0 + ROWS, :] = (y * scale1 + shift).astype(BF16)


def _causal_conv(ext_ref, w_ref, halo, taps, r0, rows, c0, c1):
    acc = None
    for k in range(taps):
        start = halo - (taps - 1) + k + r0
        term = w_ref[k:k + 1, c0:c1] * ext_ref[start:start + rows, c0:c1]
        acc = term if acc is None else acc + term
    return acc


def _adaln_kernel(c_ref, w_ref, b_ref, o_ref):
    a = _silu(c_ref[...])
    w = w_ref[...]
    a_parts = _split_bf16(a, 3)
    w_parts = _split_bf16(w, 3)
    acc = b_ref[...] + jnp.zeros(o_ref.shape, F32)
    for i in range(3):
        for j in range(3 - i):
            acc = acc + _dot(a_parts[i], w_parts[j])
    o_ref[...] = acc


def _adaln(c, w_all, b_all):
    k, d, n = w_all.shape
    bsz = c.shape[0]
    nb = D_MODEL
    return pl.pallas_call(
        _adaln_kernel,
        grid=(k, n // nb),
        in_specs=[
            pl.BlockSpec((bsz, d), lambda i, j: (0, 0)),
            pl.BlockSpec((None, d, nb), lambda i, j: (i, 0, j)),
            pl.BlockSpec((None, 1, nb), lambda i, j: (i, 0, j)),
        ],
        out_specs=pl.BlockSpec((None, bsz, nb), lambda i, j: (i, 0, j)),
        out_shape=jax.ShapeDtypeStruct((k, bsz, n), F32),
        compiler_params=pltpu.CompilerParams(
            dimension_semantics=("arbitrary", "arbitrary"), vmem_limit_bytes=VMEM_LIMIT),
        name="adaln",
    )(c, w_all, b_all)


def _ssd_chunk(r0, xbc_ref, dt_ref, z_ref, gn_ref, state_ref, dtb_ref, alog_ref, dx_ref,
               gng_ref, ltri_ref, e_ref):
    q = SSD_CHUNK
    rows = slice(r0, r0 + q)
    lane = lax.broadcasted_iota(jnp.int32, (q, LANES), 1)
    row = lax.broadcasted_iota(jnp.int32, (q, LANES), 0)
    causal = row >= lane

    dt = _softplus(dt_ref[rows, :] + dtb_ref[...])
    a = -jnp.exp(alog_ref[...])
    a_cum = _dot_split_rhs(ltri_ref[...], dt * a, 3)
    a_cum_t = a_cum.T
    dt_t = dt.T
    a_last = a_cum[q - 1:q, :]

    e = e_ref[...]
    decay_out_x = _dot_split_lhs(jnp.exp(a_cum), e, 2)
    decay_state_x = _dot_split_lhs(jnp.exp(a_last - a_cum) * dt, e, 2)

    xs = xbc_ref[rows, 0:SSD_INNER]
    bm = xbc_ref[rows, SSD_INNER:SSD_INNER + LANES]
    cm = xbc_ref[rows, SSD_INNER + LANES:SSD_XBC]
    bm_b = bm.astype(BF16)
    cm_b = cm.astype(BF16)

    cb = []
    for g in range(SSD_GROUPS):
        sl = slice(g * SSD_STATE, (g + 1) * SSD_STATE)
        cb.append(lax.dot_general(cm_b[:, sl], bm_b[:, sl], (((1,), (1,)), ((), ())),
                                  preferred_element_type=F32))

    heads_per_group = SSD_HEADS // SSD_GROUPS
    y_diag = []
    for pair in range(SSD_HEADS // 2):
        wts = []
        for hh in (2 * pair, 2 * pair + 1):
            seg = a_cum[:, hh:hh + 1] - a_cum_t[hh:hh + 1, :]
            decay = jnp.exp(jnp.where(causal, seg, -jnp.inf))
            wts.append((cb[hh // heads_per_group] * decay * dt_t[hh:hh + 1, :]).astype(BF16))
        x_pair = xs[:, pair * LANES:(pair + 1) * LANES]
        x_lo = jnp.where(lane < SSD_HEAD_DIM, x_pair, 0.0).astype(BF16)
        x_hi = jnp.where(lane >= SSD_HEAD_DIM, x_pair, 0.0).astype(BF16)
        y_diag.append(_dot(jnp.concatenate(wts, axis=1), jnp.concatenate([x_lo, x_hi], axis=0)))
    y_diag = jnp.concatenate(y_diag, axis=1)

    state = state_ref[...]
    y_off = _dot(cm_b, state.astype(BF16)) * decay_out_x

    xd = (xs * decay_state_x).astype(BF16)
    new = _dot(bm.T.astype(BF16), xd)
    srow = lax.broadcasted_iota(jnp.int32, (LANES, SSD_INNER), 0)
    scol = lax.broadcasted_iota(jnp.int32, (LANES, SSD_INNER), 1)
    same_group = (srow >= SSD_STATE) == (scol >= SSD_INNER // SSD_GROUPS)
    state_ref[...] = state * decay_out_x[q - 1:q, :] + jnp.where(same_group, new, 0.0)

    y = y_diag + y_off + xs * dx_ref[...]

    z = z_ref[rows, :]
    v = y * _silu(z)
    gw = SSD_INNER // SSD_GROUPS
    for g in range(SSD_GROUPS):
        vg = v[:, g * gw:(g + 1) * gw]
        ms = jnp.mean(vg * vg, axis=-1, keepdims=True)
        gn_ref[rows, g * gw:(g + 1) * gw] = (
            vg * lax.rsqrt(ms + EPS) * gng_ref[:, g * gw:(g + 1) * gw]).astype(BF16)


def _mixer_kernel(x_ref, mod_ref, ng_ref, win_ref, bg_ref, cws_ref, cbs_ref, dtb_ref, alog_ref,
                  dx_ref, gng_ref, wso_ref, cwc_ref, cbc_ref, lng_ref, lnb_ref, wco_ref,
                  cwsc_ref, wsco_ref, wo_ref, ltri_ref, e_ref,
                  o_ref,
                  h_ref, z_ref, xbcx_ref, xbc_ref, dt_ref, gn_ref, state_ref,
                  cs_ref, confx_ref, scx_ref, act_ref, gate_ref, m_ref):
    tile = x_ref.shape[0]
    j = pl.program_id(1)

    @pl.when(j == 0)
    def _():
        xbcx_ref[0:SMALL_HALO, :] = jnp.zeros((SMALL_HALO, SSD_XBC), F32)
        confx_ref[0:CONF_HALO, :] = jnp.zeros((CONF_HALO, CONF_WIDTH), F32)
        scx_ref[0:SMALL_HALO, :] = jnp.zeros((SMALL_HALO, SC_WIDTH), F32)
        state_ref[...] = jnp.zeros(state_ref.shape, F32)

    _modulated_norm(x_ref, mod_ref, ng_ref, h_ref, tile)
    h = h_ref[...]

    z_ref[...] = _dot(h, win_ref[:, COL_Z:COL_XBC])
    xbcx_ref[SMALL_HALO:SMALL_HALO + tile, :] = _dot(h, win_ref[:, COL_XBC:COL_DT])
    dt_ref[...] = _dot(h, win_ref[:, COL_DT:COL_CONF])

    for r0 in range(0, tile, ROWS):
        conv = _causal_conv(xbcx_ref, cws_ref, SMALL_HALO, SSD_CONV, r0, ROWS, 0, SSD_XBC)
        xbc_ref[r0:r0 + ROWS, :] = _silu(conv + cbs_ref[...])
    xbcx_ref[0:SMALL_HALO, :] = xbcx_ref[tile:tile + SMALL_HALO, :]

    for r0 in range(0, tile, SSD_CHUNK):
        _ssd_chunk(r0, xbc_ref, dt_ref, z_ref, gn_ref, state_ref, dtb_ref, alog_ref, dx_ref,
                   gng_ref, ltri_ref, e_ref)
    d = D_MODEL
    gate_ref[...] = _dot(h, win_ref[:, COL_GATE:N_IN_PAD])
    m_ref[...] = (_sigmoid(gate_ref[:, 0:d] + bg_ref[:, 0:d])
                  * _dot(gn_ref[...], wso_ref[...]))

    cs_ref[...] = _dot(h, win_ref[:, COL_CONF:COL_GATE])
    cw = CONF_WIDTH
    for r0 in range(0, tile, ROWS):
        rows = slice(r0, r0 + ROWS)
        confx_ref[CONF_HALO + r0:CONF_HALO + r0 + ROWS, :] = (
            cs_ref[rows, 0:cw] * _sigmoid(cs_ref[rows, cw:2 * cw]))
        scx_ref[SMALL_HALO + r0:SMALL_HALO + r0 + ROWS, :] = (
            cs_ref[rows, 3 * cw:4 * cw] * cs_ref[rows, 4 * cw:5 * cw])
    for r0 in range(0, tile, ROWS):
        rows = slice(r0, r0 + ROWS)
        u = _causal_conv(confx_ref, cwc_ref, CONF_HALO, CONF_KERNEL, r0, ROWS, 0, cw) + cbc_ref[...]
        mu = jnp.mean(u, axis=-1, keepdims=True)
        uc = u - mu
        var = jnp.mean(uc * uc, axis=-1, keepdims=True)
        u = uc * lax.rsqrt(var + EPS) * lng_ref[...] + lnb_ref[...]
        act_ref[rows, 0:cw] = _silu(u).astype(BF16)
        s = _causal_conv(scx_ref, cwsc_ref, SMALL_HALO, SC_KERNEL, r0, ROWS, 0, cw)
        act_ref[rows, cw:2 * cw] = (cs_ref[rows, 2 * cw:3 * cw] * s).astype(BF16)
    confx_ref[0:CONF_HALO, :] = confx_ref[tile:tile + CONF_HALO, :]
    scx_ref[0:SMALL_HALO, :] = scx_ref[tile:tile + SMALL_HALO, :]
    m_ref[...] += (_sigmoid(gate_ref[:, d:2 * d] + bg_ref[:, d:2 * d])
                   * _dot(act_ref[:, 0:cw], wco_ref[...]))
    m_ref[...] += (_sigmoid(gate_ref[:, 2 * d:3 * d] + bg_ref[:, 2 * d:3 * d])
                   * _dot(act_ref[:, cw:2 * cw], wsco_ref[...]))

    mix = _dot(m_ref[...].astype(BF16), wo_ref[...])
    o_ref[...] = x_ref[...] + mod_ref[2:3, :] * mix


def _const_spec(shape):
    zeros = (0,) * len(shape)
    return pl.BlockSpec(shape, lambda b, j: zeros, pipeline_mode=pl.Buffered(1))


def _mixer(x, mod, p, tile):
    bsz, seq, d = x.shape
    consts = [p["ng"], p["w_in"], p["bg"], p["cws"], p["cbs"], p["dtb"], p["alog"], p["dx"],
              p["gng"], p["wso"], p["cwc"], p["cbc"], p["lng"], p["lnb"], p["wco"], p["cwsc"],
              p["wsco"], p["wo"], p["ltri"], p["e"]]
    tok_spec = pl.BlockSpec((None, tile, d), lambda b, j: (b, j, 0))
    return pl.pallas_call(
        _mixer_kernel,
        grid=(bsz, seq // tile),
        in_specs=[tok_spec, pl.BlockSpec((None, 3, d), lambda b, j: (b, 0, 0))]
        + [_const_spec(a.shape) for a in consts],
        out_specs=tok_spec,
        out_shape=jax.ShapeDtypeStruct(x.shape, F32),
        scratch_shapes=[
            pltpu.VMEM((tile, d), BF16),
            pltpu.VMEM((tile, SSD_INNER), F32),
            pltpu.VMEM((SMALL_HALO + tile, SSD_XBC), F32),
            pltpu.VMEM((tile, SSD_XBC), F32),
            pltpu.VMEM((tile, LANES), F32),
            pltpu.VMEM((tile, SSD_INNER), BF16),
            pltpu.VMEM((LANES, SSD_INNER), F32),
            pltpu.VMEM((tile, COL_GATE - COL_CONF), F32),
            pltpu.VMEM((CONF_HALO + tile, CONF_WIDTH), F32),
            pltpu.VMEM((SMALL_HALO + tile, SC_WIDTH), F32),
            pltpu.VMEM((tile, 2 * CONF_WIDTH), BF16),
            pltpu.VMEM((tile, N_BRANCH * D_MODEL), F32),
            pltpu.VMEM((tile, D_MODEL), F32),
        ],
        compiler_params=pltpu.CompilerParams(
            dimension_semantics=("arbitrary", "arbitrary"), vmem_limit_bytes=VMEM_LIMIT),
        name="mixer",
    )(x, mod, *consts)


def _ffn_kernel(x_ref, mod_ref, ng_ref, wup_ref, cw_ref, cb_ref, wdn_ref, fg_ref,
                o_ref,
                h_ref, halo_ref, ext_ref, *, final_norm):
    tile = x_ref.shape[0]
    j = pl.program_id(1)

    @pl.when(j == 0)
    def _():
        halo_ref[...] = jnp.zeros(halo_ref.shape, F32)

    _modulated_norm(x_ref, mod_ref, ng_ref, h_ref, tile)
    h = h_ref[...]

    fb = FFN_BLOCK
    acc = None
    for nb in range(D_FF // fb):
        c0, c1 = 2 * fb * nb, 2 * fb * (nb + 1)
        ext = ext_ref.at[nb % 2]
        ext[0:SMALL_HALO, :] = halo_ref[:, c0:c1]
        ext[SMALL_HALO:SMALL_HALO + tile, :] = _dot(h, wup_ref[:, c0:c1])
        halo_ref[:, c0:c1] = ext[tile:tile + SMALL_HALO, :]
        acts = []
        for r0 in range(0, tile, ROWS):
            acc_c = None
            for k in range(FFN_KERNEL):
                start = SMALL_HALO - (FFN_KERNEL - 1) + k + r0
                term = cw_ref[k:k + 1, c0:c1] * ext[start:start + ROWS, :]
                acc_c = term if acc_c is None else acc_c + term
            u = acc_c + cb_ref[:, c0:c1]
            acts.append((_silu(u[:, 0:fb]) * u[:, fb:2 * fb]).astype(BF16))
        part = _dot(jnp.concatenate(acts, axis=0), wdn_ref[nb * fb:(nb + 1) * fb, :])
        acc = part if acc is None else acc + part

    y = x_ref[...] + mod_ref[2:3, :] * acc
    if final_norm:
        ms = jnp.mean(y * y, axis=-1, keepdims=True)
        y = y * lax.rsqrt(ms + EPS) * fg_ref[...]
    o_ref[...] = y


def _ffn(x, mod, p, final_g, tile, final_norm):
    bsz, seq, d = x.shape
    consts = [p["fng"], p["w_up"], p["fcw"], p["fcb"], p["w_down"], final_g]
    tok_spec = pl.BlockSpec((None, tile, d), lambda b, j: (b, j, 0))
    return pl.pallas_call(
        functools.partial(_ffn_kernel, final_norm=final_norm),
        grid=(bsz, seq // tile),
        in_specs=[tok_spec, pl.BlockSpec((None, 3, d), lambda b, j: (b, 0, 0))]
        + [_const_spec(a.shape) for a in consts],
        out_specs=tok_spec,
        out_shape=jax.ShapeDtypeStruct(x.shape, F32),
        scratch_shapes=[
            pltpu.VMEM((tile, d), BF16),
            pltpu.VMEM((SMALL_HALO, 2 * D_FF), F32),
            pltpu.VMEM((2, SMALL_HALO + tile, 2 * FFN_BLOCK), F32),
        ],
        compiler_params=pltpu.CompilerParams(
            dimension_semantics=("arbitrary", "arbitrary"), vmem_limit_bytes=VMEM_LIMIT),
        name="ffn",
    )(x, mod, *consts)


def _pad_lanes(a, width):
    return jnp.pad(a, [(0, 0)] * (a.ndim - 1) + [(0, width - a.shape[-1])])


def _interleave_ffn(a):
    lead = a.shape[:-1]
    g = a[..., :D_FF].reshape(*lead, D_FF // FFN_BLOCK, FFN_BLOCK)
    v = a[..., D_FF:].reshape(*lead, D_FF // FFN_BLOCK, FFN_BLOCK)
    return jnp.concatenate([g, v], axis=-1).reshape(*lead, 2 * D_FF)


def _ssd_constants():
    q = SSD_CHUNK
    ltri = np.tril(np.ones((q, q), np.float32))
    e = np.zeros((LANES, SSD_INNER), np.float32)
    for hh in range(SSD_HEADS):
        e[hh, hh * SSD_HEAD_DIM:(hh + 1) * SSD_HEAD_DIM] = 1.0
    return jnp.asarray(ltri, BF16), jnp.asarray(e, BF16)


def _layer_params(i, w_in, b_gate, ssd_conv_w, ssd_conv_b, ssd_dt_bias, ssd_a_log, ssd_d,
                  ssd_norm_g, w_ssd_out, conf_conv_w, conf_conv_b, conf_ln_g, conf_ln_b,
                  w_conf_out, sc_conv_w, w_sc_out, w_o, norm_mix_g, norm_ffn_g, w_up,
                  ffn_conv_w, ffn_conv_b, w_down):
    off_z = SSD_INNER
    off_xbc = off_z + SSD_XBC
    off_dt = off_xbc + SSD_HEADS
    off_conf = off_dt + 2 * CONF_WIDTH
    off_sc = off_conf + 3 * SC_WIDTH
    w = w_in[i]
    w_perm = jnp.concatenate([
        w[:, :off_xbc], _pad_lanes(w[:, off_xbc:off_dt], LANES), w[:, off_dt:]], axis=1)
    ltri, e = _ssd_constants()
    row = lambda a: a.reshape(1, -1)
    return {
        "ng": row(norm_mix_g[i]),
        "w_in": w_perm.astype(BF16),
        "bg": row(b_gate[i]),
        "cws": ssd_conv_w[i],
        "cbs": row(ssd_conv_b[i]),
        "dtb": _pad_lanes(row(ssd_dt_bias[i]), LANES),
        "alog": _pad_lanes(row(ssd_a_log[i]), LANES),
        "dx": row(jnp.repeat(ssd_d[i], SSD_HEAD_DIM)),
        "gng": row(ssd_norm_g[i]),
        "wso": w_ssd_out[i].astype(BF16),
        "cwc": conf_conv_w[i],
        "cbc": row(conf_conv_b[i]),
        "lng": row(conf_ln_g[i]),
        "lnb": row(conf_ln_b[i]),
        "wco": w_conf_out[i].astype(BF16),
        "cwsc": sc_conv_w[i],
        "wsco": w_sc_out[i].astype(BF16),
        "wo": w_o[i].astype(BF16),
        "ltri": ltri,
        "e": e,
        "fng": row(norm_ffn_g[i]),
        "w_up": _interleave_ffn(w_up[i]).astype(BF16),
        "fcw": _interleave_ffn(ffn_conv_w[i]),
        "fcb": _interleave_ffn(row(ffn_conv_b[i])),
        "w_down": w_down[i].astype(BF16),
    }


def kernel(x, c, ada_mix_w, ada_mix_b, norm_mix_g, w_in, b_gate, ssd_conv_w, ssd_conv_b, ssd_dt_bias, ssd_a_log, ssd_d, ssd_norm_g, w_ssd_out, conf_conv_w, conf_conv_b, conf_ln_g, conf_ln_b, w_conf_out, sc_conv_w, w_sc_out, w_o, ada_ffn_w, ada_ffn_b, norm_ffn_g, w_up, ffn_conv_w, ffn_conv_b, w_down, final_norm_g):
    bsz, seq, d = x.shape
    depth = w_in.shape[0]
    tile = min(SEQ_TILE, seq)
    assert seq % tile == 0 and tile % SSD_CHUNK == 0 and d == D_MODEL

    ada_w = jnp.stack([ada_mix_w, ada_ffn_w], axis=1).reshape(2 * depth, d, 3 * d)
    ada_b = jnp.stack([ada_mix_b, ada_ffn_b], axis=1).reshape(2 * depth, 1, 3 * d)
    mods = _adaln(c, ada_w, ada_b).reshape(2 * depth, bsz, 3, d)

    final_g = final_norm_g.reshape(1, d)
    for i in range(depth):
        p = _layer_params(i, w_in, b_gate, ssd_conv_w, ssd_conv_b, ssd_dt_bias, ssd_a_log,
                          ssd_d, ssd_norm_g, w_ssd_out, conf_conv_w, conf_conv_b, conf_ln_g,
                          conf_ln_b, w_conf_out, sc_conv_w, w_sc_out, w_o, norm_mix_g,
                          norm_ffn_g, w_up, ffn_conv_w, ffn_conv_b, w_down)
        x = _mixer(x, mods[2 * i], p, tile)
        x = _ffn(x, mods[2 * i + 1], p, final_g, tile, final_norm=(i == depth - 1))
    return x
```

```python
import functools

import numpy as np
import jax
import jax.numpy as jnp
from jax import lax
from jax.experimental import pallas as pl
from jax.experimental.pallas import tpu as pltpu

D_MODEL = 1024
DEPTH = 2
SSD_HEADS = 16
SSD_HEAD_DIM = 64
SSD_INNER = SSD_HEADS * SSD_HEAD_DIM
SSD_GROUPS = 2
SSD_STATE = 64
SSD_CONV = 4
SSD_CHUNK = 128
SSD_XBC = SSD_INNER + 2 * SSD_GROUPS * SSD_STATE
CONF_WIDTH = 512
CONF_KERNEL = 31
SC_WIDTH = 512
SC_KERNEL = 3
N_BRANCH = 3
D_FF = 2816
FFN_KERNEL = 3
EPS = 1e-6

LANES = 128
SUBLANES = 8
VMEM_LIMIT = 58 * 1024 * 1024

COL_Z = 0
COL_XBC = COL_Z + SSD_INNER
COL_DT = COL_XBC + SSD_XBC
COL_CONF = COL_DT + LANES
COL_SC = COL_CONF + 2 * CONF_WIDTH
COL_GATE = COL_SC + 3 * SC_WIDTH
N_IN_PAD = COL_GATE + N_BRANCH * D_MODEL

SEQ_TILE = 256
ROWS = 32
FFN_BLOCK = 256
CONF_HALO = 32
SMALL_HALO = SUBLANES

F32 = jnp.float32
BF16 = jnp.bfloat16


def _sigmoid(x):
    return 1.0 / (1.0 + jnp.exp(-x))


def _silu(x):
    return x * _sigmoid(x)


def _softplus(x):
    return jnp.maximum(x, 0.0) + jnp.log1p(jnp.exp(-jnp.abs(x)))


def _split_bf16(x, n):
    parts = []
    r = x
    for _ in range(n):
        p = r.astype(BF16)
        parts.append(p)
        r = r - p.astype(F32)
    return parts


def _dot(a, b):
    return jnp.dot(a, b, preferred_element_type=F32)


def _dot_split_lhs(x, w_bf16, n):
    acc = None
    for p in _split_bf16(x, n):
        t = _dot(p, w_bf16)
        acc = t if acc is None else acc + t
    return acc


def _dot_split_rhs(w_bf16, x, n):
    acc = None
    for p in _split_bf16(x, n):
        t = _dot(w_bf16, p)
        acc = t if acc is None else acc + t
    return acc


def _modulated_norm(x_ref, mod_ref, g_ref, h_ref, tile):
    g = g_ref[...]
    shift = mod_ref[0:1, :]
    scale1 = 1.0 + mod_ref[1:2, :]
    for r0 in range(0, tile, ROWS):
        x = x_ref[r0:r0 + ROWS, :]
        ms = jnp.mean(x * x, axis=-1, keepdims=True)
        y = x * lax.rsqrt(ms + EPS) * g
        h_ref[r0:r0 + ROWS, :] = (y * scale1 + shift).astype(BF16)


def _store_slabs(ext_ref, row0, value):
    rows = value.shape[0]
    for s in range(ext_ref.shape[0]):
        ext_ref[s, row0:row0 + rows, :] = value[:, s * LANES:(s + 1) * LANES]


def _causal_conv(ext_ref, w_ref, halo, taps, r0, rows, slabs, c0=0):
    outs = []
    for s in slabs:
        acc = None
        for k in range(taps):
            start = halo - (taps - 1) + k + r0
            term = (w_ref[k:k + 1, c0 + s * LANES:c0 + (s + 1) * LANES]
                    * ext_ref[s, start:start + rows, :])
            acc = term if acc is None else acc + term
        outs.append(acc)
    return outs[0] if len(outs) == 1 else jnp.concatenate(outs, axis=1)


def _roll_history(ext_ref, halo, tile):
    ext_ref[:, 0:halo, :] = ext_ref[:, tile:tile + halo, :]


def _zero_history(ext_ref, halo):
    ext_ref[:, 0:halo, :] = jnp.zeros((ext_ref.shape[0], halo, LANES), F32)


def _adaln_kernel(c_ref, w_ref, b_ref, o_ref):
    a = _silu(c_ref[...])
    w = w_ref[...]
    a_parts = _split_bf16(a, 3)
    w_parts = _split_bf16(w, 3)
    acc = b_ref[...] + jnp.zeros(o_ref.shape, F32)
    for i in range(3):
        for j in range(3 - i):
            acc = acc + _dot(a_parts[i], w_parts[j])
    o_ref[...] = acc


def _adaln(c, w_all, b_all):
    k, d, n = w_all.shape
    bsz = c.shape[0]
    nb = D_MODEL
    return pl.pallas_call(
        _adaln_kernel,
        grid=(k, n // nb),
        in_specs=[
            pl.BlockSpec((bsz, d), lambda i, j: (0, 0)),
            pl.BlockSpec((None, d, nb), lambda i, j: (i, 0, j)),
            pl.BlockSpec((None, 1, nb), lambda i, j: (i, 0, j)),
        ],
        out_specs=pl.BlockSpec((None, bsz, nb), lambda i, j: (i, 0, j)),
        out_shape=jax.ShapeDtypeStruct((k, bsz, n), F32),
        compiler_params=pltpu.CompilerParams(
            dimension_semantics=("arbitrary", "arbitrary"), vmem_limit_bytes=VMEM_LIMIT),
        name="adaln",
    )(c, w_all, b_all)


def _ssd_chunk(r0, xbc_ref, dt_ref, z_ref, gn_ref, state_ref, dtb_ref, alog_ref, dx_ref,
               gng_ref, ltri_ref, e_ref):
    q = SSD_CHUNK
    rows = slice(r0, r0 + q)
    lane = lax.broadcasted_iota(jnp.int32, (q, LANES), 1)
    row = lax.broadcasted_iota(jnp.int32, (q, LANES), 0)
    causal = row >= lane

    dt = _softplus(dt_ref[rows, :] + dtb_ref[...])
    a = -jnp.exp(alog_ref[...])
    a_cum = _dot_split_rhs(ltri_ref[...], dt * a, 3)
    a_cum_t = a_cum.T
    dt_t = dt.T
    a_last = a_cum[q - 1:q, :]

    e = e_ref[...]
    decay_out_x = _dot_split_lhs(jnp.exp(a_cum), e, 2)
    decay_state_x = _dot_split_lhs(jnp.exp(a_last - a_cum) * dt, e, 2)

    xs = xbc_ref[rows, 0:SSD_INNER]
    bm = xbc_ref[rows, SSD_INNER:SSD_INNER + LANES]
    cm = xbc_ref[rows, SSD_INNER + LANES:SSD_XBC]
    bm_b = bm.astype(BF16)
    cm_b = cm.astype(BF16)

    cb = []
    for g in range(SSD_GROUPS):
        sl = slice(g * SSD_STATE, (g + 1) * SSD_STATE)
        cb.append(lax.dot_general(cm_b[:, sl], bm_b[:, sl], (((1,), (1,)), ((), ())),
                                  preferred_element_type=F32))

    heads_per_group = SSD_HEADS // SSD_GROUPS
    y_diag = []
    for pair in range(SSD_HEADS // 2):
        wts = []
        for hh in (2 * pair, 2 * pair + 1):
            seg = a_cum[:, hh:hh + 1] - a_cum_t[hh:hh + 1, :]
            decay = jnp.exp(jnp.where(causal, seg, -jnp.inf))
            wts.append((cb[hh // heads_per_group] * decay * dt_t[hh:hh + 1, :]).astype(BF16))
        x_pair = xs[:, pair * LANES:(pair + 1) * LANES]
        x_lo = jnp.where(lane < SSD_HEAD_DIM, x_pair, 0.0).astype(BF16)
        x_hi = jnp.where(lane >= SSD_HEAD_DIM, x_pair, 0.0).astype(BF16)
        y_diag.append(_dot(jnp.concatenate(wts, axis=1), jnp.concatenate([x_lo, x_hi], axis=0)))
    y_diag = jnp.concatenate(y_diag, axis=1)

    state = state_ref[...]
    y_off = _dot(cm_b, state.astype(BF16)) * decay_out_x

    xd = (xs * decay_state_x).astype(BF16)
    new = _dot(bm.T.astype(BF16), xd)
    srow = lax.broadcasted_iota(jnp.int32, (LANES, SSD_INNER), 0)
    scol = lax.broadcasted_iota(jnp.int32, (LANES, SSD_INNER), 1)
    same_group = (srow >= SSD_STATE) == (scol >= SSD_INNER // SSD_GROUPS)
    state_ref[...] = state * decay_out_x[q - 1:q, :] + jnp.where(same_group, new, 0.0)

    y = y_diag + y_off + xs * dx_ref[...]

    z = z_ref[rows, :]
    v = y * _silu(z)
    gw = SSD_INNER // SSD_GROUPS
    for g in range(SSD_GROUPS):
        vg = v[:, g * gw:(g + 1) * gw]
        ms = jnp.mean(vg * vg, axis=-1, keepdims=True)
        gn_ref[rows, g * gw:(g + 1) * gw] = (
            vg * lax.rsqrt(ms + EPS) * gng_ref[:, g * gw:(g + 1) * gw]).astype(BF16)


def _mixer_kernel(x_ref, mod_ref, ng_ref, win_ref, bg_ref, cws_ref, cbs_ref, dtb_ref, alog_ref,
                  dx_ref, gng_ref, wso_ref, cwc_ref, cbc_ref, lng_ref, lnb_ref, wco_ref,
                  cwsc_ref, wsco_ref, wo_ref, ltri_ref, e_ref,
                  o_ref,
                  h_ref, z_ref, xbcx_ref, xbc_ref, dt_ref, gn_ref, state_ref,
                  cs_ref, confx_ref, scx_ref, act_ref, gate_ref, m_ref):
    tile = x_ref.shape[0]
    j = pl.program_id(1)
    d = D_MODEL
    cw = CONF_WIDTH
    conf_slabs = range(CONF_WIDTH // LANES)
    sc_slabs = range(SC_WIDTH // LANES)
    xbc_slabs = range(SSD_XBC // LANES)

    @pl.when(j == 0)
    def _():
        _zero_history(xbcx_ref, SMALL_HALO)
        _zero_history(confx_ref, CONF_HALO)
        _zero_history(scx_ref, SMALL_HALO)
        state_ref[...] = jnp.zeros(state_ref.shape, F32)

    _modulated_norm(x_ref, mod_ref, ng_ref, h_ref, tile)


    cs_ref[...] = _dot(h_ref[...], win_ref[:, COL_CONF:COL_GATE])

    z_ref[...] = _dot(h_ref[...], win_ref[:, COL_Z:COL_XBC])
    _store_slabs(xbcx_ref, SMALL_HALO, _dot(h_ref[...], win_ref[:, COL_XBC:COL_DT]))
    dt_ref[...] = _dot(h_ref[...], win_ref[:, COL_DT:COL_CONF])
    for r0 in range(0, tile, ROWS):
        rows = slice(r0, r0 + ROWS)
        _store_slabs(confx_ref, CONF_HALO + r0,
                     cs_ref[rows, 0:cw] * _sigmoid(cs_ref[rows, cw:2 * cw]))
        _store_slabs(scx_ref, SMALL_HALO + r0,
                     cs_ref[rows, 3 * cw:4 * cw] * cs_ref[rows, 4 * cw:5 * cw])

    gate_ref[...] = _dot(h_ref[...], win_ref[:, COL_GATE:N_IN_PAD])
    for r0 in range(0, tile, ROWS):
        rows = slice(r0, r0 + ROWS)
        u = _causal_conv(confx_ref, cwc_ref, CONF_HALO, CONF_KERNEL, r0, ROWS, conf_slabs)
        u = u + cbc_ref[...]
        mu = jnp.mean(u, axis=-1, keepdims=True)
        uc = u - mu
        var = jnp.mean(uc * uc, axis=-1, keepdims=True)
        u = uc * lax.rsqrt(var + EPS) * lng_ref[...] + lnb_ref[...]
        act_ref[rows, 0:cw] = _silu(u).astype(BF16)
        s = _causal_conv(scx_ref, cwsc_ref, SMALL_HALO, SC_KERNEL, r0, ROWS, sc_slabs)
        act_ref[rows, cw:2 * cw] = (cs_ref[rows, 2 * cw:3 * cw] * s).astype(BF16)
    _roll_history(confx_ref, CONF_HALO, tile)
    _roll_history(scx_ref, SMALL_HALO, tile)

    m_ref[...] = (_sigmoid(gate_ref[:, d:2 * d] + bg_ref[:, d:2 * d])
                  * _dot(act_ref[:, 0:cw], wco_ref[...]))
    m_ref[...] += (_sigmoid(gate_ref[:, 2 * d:3 * d] + bg_ref[:, 2 * d:3 * d])
                   * _dot(act_ref[:, cw:2 * cw], wsco_ref[...]))
    for r0 in range(0, tile, ROWS):
        conv = _causal_conv(xbcx_ref, cws_ref, SMALL_HALO, SSD_CONV, r0, ROWS, xbc_slabs)
        xbc_ref[r0:r0 + ROWS, :] = _silu(conv + cbs_ref[...])
    _roll_history(xbcx_ref, SMALL_HALO, tile)

    for r0 in range(0, tile, SSD_CHUNK):
        _ssd_chunk(r0, xbc_ref, dt_ref, z_ref, gn_ref, state_ref, dtb_ref, alog_ref, dx_ref,
                   gng_ref, ltri_ref, e_ref)
    m_ref[...] += (_sigmoid(gate_ref[:, 0:d] + bg_ref[:, 0:d])
                   * _dot(gn_ref[...], wso_ref[...]))

    mix = _dot(m_ref[...].astype(BF16), wo_ref[...])
    o_ref[...] = x_ref[...] + mod_ref[2:3, :] * mix


def _const_spec(shape):
    zeros = (0,) * len(shape)
    return pl.BlockSpec(shape, lambda b, j: zeros, pipeline_mode=pl.Buffered(1))


def _mixer(x, mod, p, tile):
    bsz, seq, d = x.shape
    consts = [p["ng"], p["w_in"], p["bg"], p["cws"], p["cbs"], p["dtb"], p["alog"], p["dx"],
              p["gng"], p["wso"], p["cwc"], p["cbc"], p["lng"], p["lnb"], p["wco"], p["cwsc"],
              p["wsco"], p["wo"], p["ltri"], p["e"]]
    tok_spec = pl.BlockSpec((None, tile, d), lambda b, j: (b, j, 0))
    return pl.pallas_call(
        _mixer_kernel,
        grid=(bsz, seq // tile),
        in_specs=[tok_spec, pl.BlockSpec((None, 3, d), lambda b, j: (b, 0, 0))]
        + [_const_spec(a.shape) for a in consts],
        out_specs=tok_spec,
        out_shape=jax.ShapeDtypeStruct(x.shape, F32),
        scratch_shapes=[
            pltpu.VMEM((tile, d), BF16),
            pltpu.VMEM((tile, SSD_INNER), F32),
            pltpu.VMEM((SSD_XBC // LANES, SMALL_HALO + tile, LANES), F32),
            pltpu.VMEM((tile, SSD_XBC), F32),
            pltpu.VMEM((tile, LANES), F32),
            pltpu.VMEM((tile, SSD_INNER), BF16),
            pltpu.VMEM((LANES, SSD_INNER), F32),
            pltpu.VMEM((tile, COL_GATE - COL_CONF), F32),
            pltpu.VMEM((CONF_WIDTH // LANES, CONF_HALO + tile, LANES), F32),
            pltpu.VMEM((SC_WIDTH // LANES, SMALL_HALO + tile, LANES), F32),
            pltpu.VMEM((tile, 2 * CONF_WIDTH), BF16),
            pltpu.VMEM((tile, N_BRANCH * D_MODEL), F32),
            pltpu.VMEM((tile, D_MODEL), F32),
        ],
        compiler_params=pltpu.CompilerParams(
            dimension_semantics=("arbitrary", "arbitrary"), vmem_limit_bytes=VMEM_LIMIT),
        name="mixer",
    )(x, mod, *consts)


def _ffn_kernel(x_ref, mod_ref, ng_ref, wup_ref, cw_ref, cb_ref, wdn_ref, fg_ref,
                o_ref,
                h_ref, halo_ref, ext_ref, act_ref, acc_ref, *, final_norm):
    tile = x_ref.shape[0]
    j = pl.program_id(1)

    @pl.when(j == 0)
    def _():
        halo_ref[...] = jnp.zeros(halo_ref.shape, F32)

    _modulated_norm(x_ref, mod_ref, ng_ref, h_ref, tile)

    fb = FFN_BLOCK
    n_blocks = D_FF // fb
    half = fb // LANES

    def cols(nb):
        return 2 * fb * nb, 2 * fb * (nb + 1)

    def up(nb):
        c0, c1 = cols(nb)
        ext = ext_ref.at[nb % 2]
        _store_slabs(ext, 0, halo_ref[:, c0:c1])
        u = _dot(h_ref[...], wup_ref[:, c0:c1])
        _store_slabs(ext, SMALL_HALO, u)
        halo_ref[:, c0:c1] = u[tile - SMALL_HALO:tile, :]

    def activate(nb):
        c0, _ = cols(nb)
        ext = ext_ref.at[nb % 2]
        for r0 in range(0, tile, ROWS):
            for s in range(half):
                g, v = (_causal_conv(ext, cw_ref, SMALL_HALO, FFN_KERNEL, r0, ROWS, [t], c0)
                        + cb_ref[:, c0 + t * LANES:c0 + (t + 1) * LANES]
                        for t in (s, s + half))
                act_ref[nb % 2, r0:r0 + ROWS, s * LANES:(s + 1) * LANES] = (
                    _silu(g) * v).astype(BF16)

    def down(nb):
        part = _dot(act_ref[nb % 2], wdn_ref[nb * fb:(nb + 1) * fb, :])
        if nb == 0:
            acc_ref[...] = part
        else:
            acc_ref[...] += part

    up(0)
    for nb in range(n_blocks):
        if nb + 1 < n_blocks:
            up(nb + 1)
        if nb >= 1:
            down(nb - 1)
        activate(nb)
    down(n_blocks - 1)

    gate = mod_ref[2:3, :]
    for r0 in range(0, tile, ROWS):
        y = x_ref[r0:r0 + ROWS, :] + gate * acc_ref[r0:r0 + ROWS, :]
        if final_norm:
            ms = jnp.mean(y * y, axis=-1, keepdims=True)
            y = y * lax.rsqrt(ms + EPS) * fg_ref[...]
        o_ref[r0:r0 + ROWS, :] = y


def _ffn(x, mod, p, final_g, tile, final_norm):
    bsz, seq, d = x.shape
    consts = [p["fng"], p["w_up"], p["fcw"], p["fcb"], p["w_down"], final_g]
    tok_spec = pl.BlockSpec((None, tile, d), lambda b, j: (b, j, 0))
    return pl.pallas_call(
        functools.partial(_ffn_kernel, final_norm=final_norm),
        grid=(bsz, seq // tile),
        in_specs=[tok_spec, pl.BlockSpec((None, 3, d), lambda b, j: (b, 0, 0))]
        + [_const_spec(a.shape) for a in consts],
        out_specs=tok_spec,
        out_shape=jax.ShapeDtypeStruct(x.shape, F32),
        scratch_shapes=[
            pltpu.VMEM((tile, d), BF16),
            pltpu.VMEM((SMALL_HALO, 2 * D_FF), F32),
            pltpu.VMEM((2, 2 * FFN_BLOCK // LANES, SMALL_HALO + tile, LANES), F32),
            pltpu.VMEM((2, tile, FFN_BLOCK), BF16),
            pltpu.VMEM((tile, d), F32),
        ],
        compiler_params=pltpu.CompilerParams(
            dimension_semantics=("arbitrary", "arbitrary"), vmem_limit_bytes=VMEM_LIMIT),
        name="ffn",
    )(x, mod, *consts)


def _pad_lanes(a, width):
    return jnp.pad(a, [(0, 0)] * (a.ndim - 1) + [(0, width - a.shape[-1])])


def _interleave_ffn(a):
    lead = a.shape[:-1]
    g = a[..., :D_FF].reshape(*lead, D_FF // FFN_BLOCK, FFN_BLOCK)
    v = a[..., D_FF:].reshape(*lead, D_FF // FFN_BLOCK, FFN_BLOCK)
    return jnp.concatenate([g, v], axis=-1).reshape(*lead, 2 * D_FF)


def _ssd_constants():
    q = SSD_CHUNK
    ltri = np.tril(np.ones((q, q), np.float32))
    e = np.zeros((LANES, SSD_INNER), np.float32)
    for hh in range(SSD_HEADS):
        e[hh, hh * SSD_HEAD_DIM:(hh + 1) * SSD_HEAD_DIM] = 1.0
    return jnp.asarray(ltri, BF16), jnp.asarray(e, BF16)


def _layer_params(i, w_in, b_gate, ssd_conv_w, ssd_conv_b, ssd_dt_bias, ssd_a_log, ssd_d,
                  ssd_norm_g, w_ssd_out, conf_conv_w, conf_conv_b, conf_ln_g, conf_ln_b,
                  w_conf_out, sc_conv_w, w_sc_out, w_o, norm_mix_g, norm_ffn_g, w_up,
                  ffn_conv_w, ffn_conv_b, w_down):
    off_z = SSD_INNER
    off_xbc = off_z + SSD_XBC
    off_dt = off_xbc + SSD_HEADS
    w = w_in[i]
    w_perm = jnp.concatenate([
        w[:, :off_xbc], _pad_lanes(w[:, off_xbc:off_dt], LANES), w[:, off_dt:]], axis=1)
    ltri, e = _ssd_constants()
    row = lambda a: a.reshape(1, -1)
    return {
        "ng": row(norm_mix_g[i]),
        "w_in": w_perm.astype(BF16),
        "bg": row(b_gate[i]),
        "cws": ssd_conv_w[i],
        "cbs": row(ssd_conv_b[i]),
        "dtb": _pad_lanes(row(ssd_dt_bias[i]), LANES),
        "alog": _pad_lanes(row(ssd_a_log[i]), LANES),
        "dx": row(jnp.repeat(ssd_d[i], SSD_HEAD_DIM)),
        "gng": row(ssd_norm_g[i]),
        "wso": w_ssd_out[i].astype(BF16),
        "cwc": conf_conv_w[i],
        "cbc": row(conf_conv_b[i]),
        "lng": row(conf_ln_g[i]),
        "lnb": row(conf_ln_b[i]),
        "wco": w_conf_out[i].astype(BF16),
        "cwsc": sc_conv_w[i],
        "wsco": w_sc_out[i].astype(BF16),
        "wo": w_o[i].astype(BF16),
        "ltri": ltri,
        "e": e,
        "fng": row(norm_ffn_g[i]),
        "w_up": _interleave_ffn(w_up[i]).astype(BF16),
        "fcw": _interleave_ffn(ffn_conv_w[i]),
        "fcb": _interleave_ffn(row(ffn_conv_b[i])),
        "w_down": w_down[i].astype(BF16),
    }


def kernel(x, c, ada_mix_w, ada_mix_b, norm_mix_g, w_in, b_gate, ssd_conv_w, ssd_conv_b, ssd_dt_bias, ssd_a_log, ssd_d, ssd_norm_g, w_ssd_out, conf_conv_w, conf_conv_b, conf_ln_g, conf_ln_b, w_conf_out, sc_conv_w, w_sc_out, w_o, ada_ffn_w, ada_ffn_b, norm_ffn_g, w_up, ffn_conv_w, ffn_conv_b, w_down, final_norm_g):
    bsz, seq, d = x.shape
    depth = w_in.shape[0]
    tile = min(SEQ_TILE, seq)
    assert seq % tile == 0 and tile % SSD_CHUNK == 0 and d == D_MODEL

    ada_w = jnp.stack([ada_mix_w, ada_ffn_w], axis=1).reshape(2 * depth, d, 3 * d)
    ada_b = jnp.stack([ada_mix_b, ada_ffn_b], axis=1).reshape(2 * depth, 1, 3 * d)
    mods = _adaln(c, ada_w, ada_b).reshape(2 * depth, bsz, 3, d)

    final_g = final_norm_g.reshape(1, d)
    for i in range(depth):
        p = _layer_params(i, w_in, b_gate, ssd_conv_w, ssd_conv_b, ssd_dt_bias, ssd_a_log,
                          ssd_d, ssd_norm_g, w_ssd_out, conf_conv_w, conf_conv_b, conf_ln_g,
                          conf_ln_b, w_conf_out, sc_conv_w, w_sc_out, w_o, norm_mix_g,
                          norm_ffn_g, w_up, ffn_conv_w, ffn_conv_b, w_down)
        x = _mixer(x, mods[2 * i], p, tile)
        x = _ffn(x, mods[2 * i + 1], p, final_g, tile, final_norm=(i == depth - 1))
    return x
```

```python
import functools

import numpy as np
import jax
import jax.numpy as jnp
from jax import lax
from jax.experimental import pallas as pl
from jax.experimental.pallas import tpu as pltpu

D_MODEL = 1024
SSD_HEADS = 16
SSD_HEAD_DIM = 64
SSD_INNER = SSD_HEADS * SSD_HEAD_DIM
SSD_GROUPS = 2
SSD_STATE = 64
SSD_CONV = 4
SSD_CHUNK = 128
SSD_XBC = SSD_INNER + 2 * SSD_GROUPS * SSD_STATE
CONF_WIDTH = 512
CONF_KERNEL = 31
SC_WIDTH = 512
SC_KERNEL = 3
N_BRANCH = 3
D_FF = 2816
FFN_KERNEL = 3
EPS = 1e-6

LANES = 128
SUBLANES = 8
VMEM_LIMIT = 58 * 1024 * 1024

COL_Z = 0
COL_XBC = COL_Z + SSD_INNER
COL_DT = COL_XBC + SSD_XBC
N_IN_A = COL_DT + LANES
COL_CONF = 0
COL_SC = COL_CONF + 2 * CONF_WIDTH
COL_GATE = COL_SC + 3 * SC_WIDTH
N_IN_B = COL_GATE + N_BRANCH * D_MODEL

SEQ_TILE = 512
SUB_TILE = 256
ROWS = 32
FFN_BLOCK = 256
CONF_HALO = 32
SMALL_HALO = SUBLANES

F32 = jnp.float32
BF16 = jnp.bfloat16

_dot = functools.partial(jnp.dot, preferred_element_type=F32)


def _sigmoid(x):
    return 1.0 / (1.0 + jnp.exp(-x))


def _silu(x):
    return x * _sigmoid(x)


def _softplus(x):
    return jnp.maximum(x, 0.0) + jnp.log1p(jnp.exp(-jnp.abs(x)))


def _split_bf16(x, n):
    parts = []
    r = x
    for _ in range(n):
        p = r.astype(BF16)
        parts.append(p)
        r = r - p.astype(F32)
    return parts


def _dot_split_rhs(w_bf16, x, n):
    acc = None
    for p in _split_bf16(x, n):
        t = _dot(w_bf16, p)
        acc = t if acc is None else acc + t
    return acc


def _modulated_norm(x_ref, mod_ref, g_ref, h_ref, base, tile):
    g = g_ref[...]
    shift = mod_ref[0:1, :]
    scale1 = 1.0 + mod_ref[1:2, :]
    for r0 in range(0, tile, ROWS):
        x = x_ref[pl.ds(base + r0, ROWS), :]
        ms = jnp.mean(x * x, axis=-1, keepdims=True)
        y = x * lax.rsqrt(ms + EPS) * g
        h_ref[r0:r0 + ROWS, :] = (y * scale1 + shift).astype(BF16)


def _store_slabs(ext_ref, row0, value):
    rows = value.shape[0]
    for s in range(ext_ref.shape[0]):
        ext_ref[s, row0:row0 + rows, :] = value[:, s * LANES:(s + 1) * LANES]


def _causal_conv(ext_ref, w_ref, halo, taps, r0, rows, slabs, cols):
    outs = []
    for s, c in zip(slabs, cols):
        acc = None
        for k in range(taps):
            start = halo - (taps - 1) + k + r0
            term = w_ref[k:k + 1, c:c + LANES] * ext_ref[s, start:start + rows, :]
            acc = term if acc is None else acc + term
        outs.append(acc)
    return outs[0] if len(outs) == 1 else jnp.concatenate(outs, axis=1)


def _slab_cols(n):
    return [s * LANES for s in range(n)]


def _roll_history(ext_ref, halo, tile):
    ext_ref[:, 0:halo, :] = ext_ref[:, tile:tile + halo, :]


def _zero_history(ext_ref, halo):
    ext_ref[:, 0:halo, :] = jnp.zeros((ext_ref.shape[0], halo, LANES), F32)


def _adaln_kernel(c_ref, w_ref, b_ref, o_ref):
    a = _silu(c_ref[...])
    w = w_ref[...]
    a_parts = _split_bf16(a, 3)
    w_parts = _split_bf16(w, 3)
    acc = b_ref[...] + jnp.zeros(o_ref.shape, F32)
    for i in range(3):
        for j in range(3 - i):
            acc = acc + _dot(a_parts[i], w_parts[j])
    o_ref[...] = acc


def _adaln(c, w_all, b_all):
    k, d, n = w_all.shape
    bsz = c.shape[0]
    nb = D_MODEL
    return pl.pallas_call(
        _adaln_kernel,
        grid=(k, n // nb),
        in_specs=[
            pl.BlockSpec((bsz, d), lambda i, j: (0, 0)),
            pl.BlockSpec((None, d, nb), lambda i, j: (i, 0, j)),
            pl.BlockSpec((None, 1, nb), lambda i, j: (i, 0, j)),
        ],
        out_specs=pl.BlockSpec((None, bsz, nb), lambda i, j: (i, 0, j)),
        out_shape=jax.ShapeDtypeStruct((k, bsz, n), F32),
        compiler_params=pltpu.CompilerParams(
            dimension_semantics=("arbitrary", "arbitrary"), vmem_limit_bytes=VMEM_LIMIT),
        name="adaln",
    )(c, w_all, b_all.reshape(k, 1, n))


def _ssd_chunk(r0, xbc_ref, dt_ref, z_ref, gn_ref, state_ref, dtb_ref, alog_ref, dx_ref,
               gng_ref, ltri_ref, e_ref):
    q = SSD_CHUNK
    rows = slice(r0, r0 + q)
    lane = lax.broadcasted_iota(jnp.int32, (q, LANES), 1)
    row = lax.broadcasted_iota(jnp.int32, (q, LANES), 0)
    causal = row >= lane

    dt = _softplus(dt_ref[rows, :] + dtb_ref[...])
    a = -jnp.exp(alog_ref[...])
    a_cum = _dot_split_rhs(ltri_ref[...], dt * a, 3)
    a_cum_t = a_cum.T
    dt_t = dt.T
    a_last = a_cum[q - 1:q, :]

    e2 = e_ref[...]
    decay_out_x = _dot(jnp.concatenate(_split_bf16(jnp.exp(a_cum), 2), axis=1), e2)
    decay_state_x = _dot(
        jnp.concatenate(_split_bf16(jnp.exp(a_last - a_cum) * dt, 2), axis=1), e2)

    xs = xbc_ref[rows, 0:SSD_INNER]
    bm = xbc_ref[rows, SSD_INNER:SSD_INNER + LANES]
    cm = xbc_ref[rows, SSD_INNER + LANES:SSD_XBC]
    bm_b = bm.astype(BF16)
    cm_b = cm.astype(BF16)

    cb = []
    for g in range(SSD_GROUPS):
        sl = slice(g * SSD_STATE, (g + 1) * SSD_STATE)
        cb.append(lax.dot_general(cm_b[:, sl], bm_b[:, sl], (((1,), (1,)), ((), ())),
                                  preferred_element_type=F32))

    heads_per_group = SSD_HEADS // SSD_GROUPS
    y_diag = []
    for pair in range(SSD_HEADS // 2):
        wts = []
        for hh in (2 * pair, 2 * pair + 1):
            seg = a_cum[:, hh:hh + 1] - a_cum_t[hh:hh + 1, :]
            decay = jnp.exp(jnp.where(causal, seg, -jnp.inf))
            wts.append((cb[hh // heads_per_group] * decay * dt_t[hh:hh + 1, :]).astype(BF16))
        x_pair = xs[:, pair * LANES:(pair + 1) * LANES]
        x_lo = jnp.where(lane < SSD_HEAD_DIM, x_pair, 0.0).astype(BF16)
        x_hi = jnp.where(lane >= SSD_HEAD_DIM, x_pair, 0.0).astype(BF16)
        y_diag.append(_dot(jnp.concatenate(wts, axis=1), jnp.concatenate([x_lo, x_hi], axis=0)))
    y_diag = jnp.concatenate(y_diag, axis=1)

    state = state_ref[...]
    y_off = _dot(cm_b, state.astype(BF16)) * decay_out_x

    xd = (xs * decay_state_x).astype(BF16)
    new = _dot(bm.T.astype(BF16), xd)
    srow = lax.broadcasted_iota(jnp.int32, (LANES, SSD_INNER), 0)
    scol = lax.broadcasted_iota(jnp.int32, (LANES, SSD_INNER), 1)
    same_group = (srow >= SSD_STATE) == (scol >= SSD_INNER // SSD_GROUPS)
    state_ref[...] = state * decay_out_x[q - 1:q, :] + jnp.where(same_group, new, 0.0)

    y = y_diag + y_off + xs * dx_ref[...]

    z = z_ref[rows, :]
    v = y * _silu(z)
    gw = SSD_INNER // SSD_GROUPS
    for g in range(SSD_GROUPS):
        vg = v[:, g * gw:(g + 1) * gw]
        ms = jnp.mean(vg * vg, axis=-1, keepdims=True)
        gn_ref[rows, g * gw:(g + 1) * gw] = (
            vg * lax.rsqrt(ms + EPS) * gng_ref[:, g * gw:(g + 1) * gw]).astype(BF16)


def _mixer_kernel(x_ref, mod_ref, ng_ref, wina_ref, winb_ref, bg_ref, cws_ref, cbs_ref,
                  dtb_ref, alog_ref, dx_ref, gng_ref, wso_ref, cwc_ref, cbc_ref, lng_ref,
                  lnb_ref, wco_ref, cwsc_ref, wsco_ref, wo_ref, ltri_ref, e_ref,
                  o_ref,
                  h_ref, z_ref, xbcx_ref, xbc_ref, dt_ref, gn_ref, state_ref,
                  cs_ref, confx_ref, scx_ref, act_ref, gate_ref, m_ref):
    tile = SUB_TILE
    d = D_MODEL
    cw = CONF_WIDTH
    conf_slabs = range(CONF_WIDTH // LANES)
    sc_slabs = range(SC_WIDTH // LANES)
    xbc_slabs = range(SSD_XBC // LANES)

    @pl.when(pl.program_id(1) == 0)
    def _():
        _zero_history(xbcx_ref, SMALL_HALO)
        _zero_history(confx_ref, CONF_HALO)
        _zero_history(scx_ref, SMALL_HALO)
        state_ref[...] = jnp.zeros(state_ref.shape, F32)

    def sub_tile(i, carry):
        base = pl.multiple_of(i * tile, tile)
        _modulated_norm(x_ref, mod_ref, ng_ref, h_ref, base, tile)


        cs_ref[...] = _dot(h_ref[...], winb_ref[:, COL_CONF:COL_GATE])

        z_ref[...] = _dot(h_ref[...], wina_ref[:, COL_Z:COL_XBC])
        _store_slabs(xbcx_ref, SMALL_HALO, _dot(h_ref[...], wina_ref[:, COL_XBC:COL_DT]))
        dt_ref[...] = _dot(h_ref[...], wina_ref[:, COL_DT:N_IN_A])
        for r0 in range(0, tile, ROWS):
            rows = slice(r0, r0 + ROWS)
            _store_slabs(confx_ref, CONF_HALO + r0,
                         cs_ref[rows, 0:cw] * _sigmoid(cs_ref[rows, cw:2 * cw]))
            _store_slabs(scx_ref, SMALL_HALO + r0,
                         cs_ref[rows, 3 * cw:4 * cw] * cs_ref[rows, 4 * cw:5 * cw])

        gate_ref[...] = _dot(h_ref[...], winb_ref[:, COL_GATE:N_IN_B])
        for r0 in range(0, tile, ROWS):
            rows = slice(r0, r0 + ROWS)
            u = _causal_conv(confx_ref, cwc_ref, CONF_HALO, CONF_KERNEL, r0, ROWS,
                             conf_slabs, _slab_cols(len(conf_slabs)))
            u = u + cbc_ref[...]
            mu = jnp.mean(u, axis=-1, keepdims=True)
            uc = u - mu
            var = jnp.mean(uc * uc, axis=-1, keepdims=True)
            u = uc * lax.rsqrt(var + EPS) * lng_ref[...] + lnb_ref[...]
            act_ref[rows, 0:cw] = _silu(u).astype(BF16)
            s = _causal_conv(scx_ref, cwsc_ref, SMALL_HALO, SC_KERNEL, r0, ROWS,
                             sc_slabs, _slab_cols(len(sc_slabs)))
            act_ref[rows, cw:2 * cw] = (cs_ref[rows, 2 * cw:3 * cw] * s).astype(BF16)
        _roll_history(confx_ref, CONF_HALO, tile)
        _roll_history(scx_ref, SMALL_HALO, tile)

        m_ref[...] = (_sigmoid(gate_ref[:, d:2 * d] + bg_ref[:, d:2 * d])
                      * _dot(act_ref[:, 0:cw], wco_ref[...]))
        m_ref[...] += (_sigmoid(gate_ref[:, 2 * d:3 * d] + bg_ref[:, 2 * d:3 * d])
                       * _dot(act_ref[:, cw:2 * cw], wsco_ref[...]))
        for r0 in range(0, tile, ROWS):
            conv = _causal_conv(xbcx_ref, cws_ref, SMALL_HALO, SSD_CONV, r0, ROWS,
                                xbc_slabs, _slab_cols(len(xbc_slabs)))
            xbc_ref[r0:r0 + ROWS, :] = _silu(conv + cbs_ref[...])
        _roll_history(xbcx_ref, SMALL_HALO, tile)

        for r0 in range(0, tile, SSD_CHUNK):
            _ssd_chunk(r0, xbc_ref, dt_ref, z_ref, gn_ref, state_ref, dtb_ref, alog_ref,
                       dx_ref, gng_ref, ltri_ref, e_ref)
        m_ref[...] += (_sigmoid(gate_ref[:, 0:d] + bg_ref[:, 0:d])
                       * _dot(gn_ref[...], wso_ref[...]))

        mix = _dot(m_ref[...].astype(BF16), wo_ref[...])
        rows = pl.ds(base, tile)
        o_ref[rows, :] = x_ref[rows, :] + mod_ref[2:3, :] * mix
        return carry

    lax.fori_loop(0, x_ref.shape[0] // tile, sub_tile, 0)


def _const_spec(shape):
    zeros = (0,) * len(shape)
    return pl.BlockSpec(shape, lambda b, j: zeros, pipeline_mode=pl.Buffered(1))


def _mixer(x, mod, p, seq_tile):
    bsz, seq, d = x.shape
    tile = SUB_TILE
    consts = [p["ng"], p["w_in_a"], p["w_in_b"], p["bg"], p["cws"], p["cbs"], p["dtb"],
              p["alog"], p["dx"], p["gng"], p["wso"], p["cwc"], p["cbc"], p["lng"], p["lnb"],
              p["wco"], p["cwsc"], p["wsco"], p["wo"], p["ltri"], p["e"]]
    tok_spec = pl.BlockSpec((None, seq_tile, d), lambda b, j: (b, j, 0))
    return pl.pallas_call(
        _mixer_kernel,
        grid=(bsz, seq // seq_tile),
        in_specs=[tok_spec, pl.BlockSpec((None, 3, d), lambda b, j: (b, 0, 0))]
        + [_const_spec(a.shape) for a in consts],
        out_specs=tok_spec,
        out_shape=jax.ShapeDtypeStruct(x.shape, F32),
        scratch_shapes=[
            pltpu.VMEM((tile, d), BF16),
            pltpu.VMEM((tile, SSD_INNER), F32),
            pltpu.VMEM((SSD_XBC // LANES, SMALL_HALO + tile, LANES), F32),
            pltpu.VMEM((tile, SSD_XBC), F32),
            pltpu.VMEM((tile, LANES), F32),
            pltpu.VMEM((tile, SSD_INNER), BF16),
            pltpu.VMEM((LANES, SSD_INNER), F32),
            pltpu.VMEM((tile, COL_GATE - COL_CONF), F32),
            pltpu.VMEM((CONF_WIDTH // LANES, CONF_HALO + tile, LANES), F32),
            pltpu.VMEM((SC_WIDTH // LANES, SMALL_HALO + tile, LANES), F32),
            pltpu.VMEM((tile, 2 * CONF_WIDTH), BF16),
            pltpu.VMEM((tile, N_BRANCH * D_MODEL), F32),
            pltpu.VMEM((tile, D_MODEL), F32),
        ],
        compiler_params=pltpu.CompilerParams(
            dimension_semantics=("arbitrary", "arbitrary"), vmem_limit_bytes=VMEM_LIMIT),
        name="mixer",
    )(x, mod, *consts)


def _ffn_kernel(x_ref, mod_ref, ng_ref, wup_ref, cw_ref, cb_ref, wdn_ref, fg_ref,
                o_ref,
                h_ref, halo_ref, ext_ref, act_ref, acc_ref, *, final_norm):
    tile = SUB_TILE
    fb = FFN_BLOCK
    n_blocks = D_FF // fb
    half = fb // LANES

    @pl.when(pl.program_id(1) == 0)
    def _():
        halo_ref[...] = jnp.zeros(halo_ref.shape, F32)

    def gate_value(ref, nb, row=slice(None)):
        g0, v0 = nb * fb, D_FF + nb * fb
        return jnp.concatenate([ref[row, g0:g0 + fb], ref[row, v0:v0 + fb]], axis=1)

    def up(nb):
        ext = ext_ref.at[nb % 2]
        _store_slabs(ext, 0, halo_ref[nb])
        u = _dot(h_ref[...], gate_value(wup_ref, nb))
        _store_slabs(ext, SMALL_HALO, u)
        halo_ref[nb] = u[tile - SMALL_HALO:tile, :]

    def activate(nb):
        ext = ext_ref.at[nb % 2]
        g0, v0 = nb * fb, D_FF + nb * fb
        for r0 in range(0, tile, ROWS):
            for s in range(half):
                g = _causal_conv(ext, cw_ref, SMALL_HALO, FFN_KERNEL, r0, ROWS,
                                 [s], [g0 + s * LANES])
                v = _causal_conv(ext, cw_ref, SMALL_HALO, FFN_KERNEL, r0, ROWS,
                                 [half + s], [v0 + s * LANES])
                g = g + cb_ref[:, g0 + s * LANES:g0 + (s + 1) * LANES]
                v = v + cb_ref[:, v0 + s * LANES:v0 + (s + 1) * LANES]
                act_ref[nb % 2, r0:r0 + ROWS, s * LANES:(s + 1) * LANES] = (
                    _silu(g) * v).astype(BF16)

    def down(nb):
        part = _dot(act_ref[nb % 2], wdn_ref[nb * fb:(nb + 1) * fb, :])
        if nb == 0:
            acc_ref[...] = part
        else:
            acc_ref[...] += part

    def sub_tile(i, carry):
        base = pl.multiple_of(i * tile, tile)
        _modulated_norm(x_ref, mod_ref, ng_ref, h_ref, base, tile)

        up(0)
        for nb in range(n_blocks):
            if nb + 1 < n_blocks:
                up(nb + 1)
            if nb >= 1:
                down(nb - 1)
            activate(nb)
        down(n_blocks - 1)

        gate = mod_ref[2:3, :]
        for r0 in range(0, tile, ROWS):
            rows = pl.ds(base + r0, ROWS)
            y = x_ref[rows, :] + gate * acc_ref[r0:r0 + ROWS, :]
            if final_norm:
                ms = jnp.mean(y * y, axis=-1, keepdims=True)
                y = y * lax.rsqrt(ms + EPS) * fg_ref[...]
            o_ref[rows, :] = y
        return carry

    lax.fori_loop(0, x_ref.shape[0] // tile, sub_tile, 0)


def _ffn(x, mod, p, final_g, seq_tile, final_norm):
    bsz, seq, d = x.shape
    tile = SUB_TILE
    consts = [p["fng"], p["w_up"], p["fcw"], p["fcb"], p["w_down"], final_g]
    tok_spec = pl.BlockSpec((None, seq_tile, d), lambda b, j: (b, j, 0))
    return pl.pallas_call(
        functools.partial(_ffn_kernel, final_norm=final_norm),
        grid=(bsz, seq // seq_tile),
        in_specs=[tok_spec, pl.BlockSpec((None, 3, d), lambda b, j: (b, 0, 0))]
        + [_const_spec(a.shape) for a in consts],
        out_specs=tok_spec,
        out_shape=jax.ShapeDtypeStruct(x.shape, F32),
        scratch_shapes=[
            pltpu.VMEM((tile, d), BF16),
            pltpu.VMEM((D_FF // FFN_BLOCK, SMALL_HALO, 2 * FFN_BLOCK), F32),
            pltpu.VMEM((2, 2 * FFN_BLOCK // LANES, SMALL_HALO + tile, LANES), F32),
            pltpu.VMEM((2, tile, FFN_BLOCK), BF16),
            pltpu.VMEM((tile, d), F32),
        ],
        compiler_params=pltpu.CompilerParams(
            dimension_semantics=("arbitrary", "arbitrary"), vmem_limit_bytes=VMEM_LIMIT),
        name="ffn",
    )(x, mod, *consts)


def _pad_lanes(a, width):
    return jnp.pad(a, [(0, 0)] * (a.ndim - 1) + [(0, width - a.shape[-1])])


def _ssd_constants():
    q = SSD_CHUNK
    ltri = np.tril(np.ones((q, q), np.float32))
    e = np.zeros((LANES, SSD_INNER), np.float32)
    for hh in range(SSD_HEADS):
        e[hh, hh * SSD_HEAD_DIM:(hh + 1) * SSD_HEAD_DIM] = 1.0
    return jnp.asarray(ltri, BF16), jnp.asarray(np.concatenate([e, e], axis=0), BF16)


def _layer_params(i, w_in, b_gate, ssd_conv_w, ssd_conv_b, ssd_dt_bias, ssd_a_log, ssd_d,
                  ssd_norm_g, w_ssd_out, conf_conv_w, conf_conv_b, conf_ln_g, conf_ln_b,
                  w_conf_out, sc_conv_w, w_sc_out, w_o, norm_mix_g, norm_ffn_g, w_up,
                  ffn_conv_w, ffn_conv_b, w_down):
    n_a = SSD_INNER + SSD_XBC + SSD_HEADS
    w = w_in[i]
    ltri, e = _ssd_constants()
    row = lambda a: a.reshape(1, -1)
    return {
        "ng": row(norm_mix_g[i]),
        "w_in_a": _pad_lanes(w[:, :n_a], N_IN_A).astype(BF16),
        "w_in_b": w[:, n_a:].astype(BF16),
        "bg": row(b_gate[i]),
        "cws": ssd_conv_w[i],
        "cbs": row(ssd_conv_b[i]),
        "dtb": _pad_lanes(row(ssd_dt_bias[i]), LANES),
        "alog": _pad_lanes(row(ssd_a_log[i]), LANES),
        "dx": row(jnp.repeat(ssd_d[i], SSD_HEAD_DIM)),
        "gng": row(ssd_norm_g[i]),
        "wso": w_ssd_out[i].astype(BF16),
        "cwc": conf_conv_w[i],
        "cbc": row(conf_conv_b[i]),
        "lng": row(conf_ln_g[i]),
        "lnb": row(conf_ln_b[i]),
        "wco": w_conf_out[i].astype(BF16),
        "cwsc": sc_conv_w[i],
        "wsco": w_sc_out[i].astype(BF16),
        "wo": w_o[i].astype(BF16),
        "ltri": ltri,
        "e": e,
        "fng": row(norm_ffn_g[i]),
        "w_up": w_up[i].astype(BF16),
        "fcw": ffn_conv_w[i],
        "fcb": row(ffn_conv_b[i]),
        "w_down": w_down[i].astype(BF16),
    }


def kernel(x, c, ada_mix_w, ada_mix_b, norm_mix_g, w_in, b_gate, ssd_conv_w, ssd_conv_b, ssd_dt_bias, ssd_a_log, ssd_d, ssd_norm_g, w_ssd_out, conf_conv_w, conf_conv_b, conf_ln_g, conf_ln_b, w_conf_out, sc_conv_w, w_sc_out, w_o, ada_ffn_w, ada_ffn_b, norm_ffn_g, w_up, ffn_conv_w, ffn_conv_b, w_down, final_norm_g):
    bsz, seq, d = x.shape
    depth = w_in.shape[0]
    seq_tile = min(SEQ_TILE, seq)
    assert seq % seq_tile == 0 and seq_tile % SUB_TILE == 0 and d == D_MODEL
    assert SUB_TILE % SSD_CHUNK == 0

    mods_mix = _adaln(c, ada_mix_w, ada_mix_b).reshape(depth, bsz, 3, d)
    mods_ffn = _adaln(c, ada_ffn_w, ada_ffn_b).reshape(depth, bsz, 3, d)

    final_g = final_norm_g.reshape(1, d)
    for i in range(depth):
        p = _layer_params(i, w_in, b_gate, ssd_conv_w, ssd_conv_b, ssd_dt_bias, ssd_a_log,
                          ssd_d, ssd_norm_g, w_ssd_out, conf_conv_w, conf_conv_b, conf_ln_g,
                          conf_ln_b, w_conf_out, sc_conv_w, w_sc_out, w_o, norm_mix_g,
                          norm_ffn_g, w_up, ffn_conv_w, ffn_conv_b, w_down)
        x = _mixer(x, mods_mix[i], p, seq_tile)
        x = _ffn(x, mods_ffn[i], p, final_g, seq_tile, final_norm=(i == depth - 1))
    return x
```

```python
import functools

import numpy as np
import jax
import jax.numpy as jnp
from jax import lax
from jax.experimental import pallas as pl
from jax.experimental.pallas import tpu as pltpu

D_MODEL = 1024
SSD_HEADS = 16
SSD_HEAD_DIM = 64
SSD_INNER = SSD_HEADS * SSD_HEAD_DIM
SSD_GROUPS = 2
SSD_STATE = 64
SSD_CONV = 4
SSD_CHUNK = 128
SSD_XBC = SSD_INNER + 2 * SSD_GROUPS * SSD_STATE
CONF_WIDTH = 512
CONF_KERNEL = 31
SC_WIDTH = 512
SC_KERNEL = 3
N_BRANCH = 3
D_FF = 2816
FFN_KERNEL = 3
EPS = 1e-6

LANES = 128
SUBLANES = 8
VMEM_LIMIT = 58 * 1024 * 1024

COL_Z = 0
COL_XBC = COL_Z + SSD_INNER
COL_DT = COL_XBC + SSD_XBC
N_IN_A = COL_DT + LANES
COL_CONF = 0
COL_SC = COL_CONF + 2 * CONF_WIDTH
COL_GATE = COL_SC + 3 * SC_WIDTH
N_IN_B = COL_GATE + N_BRANCH * D_MODEL

SEQ_TILE = 1024
SUB_TILE = 256
ROWS = 32
FFN_BLOCK = 256
CONF_HALO = 32
SMALL_HALO = SUBLANES

F32 = jnp.float32
BF16 = jnp.bfloat16

_dot = functools.partial(jnp.dot, preferred_element_type=F32)


def _sigmoid(x):
    return 1.0 / (1.0 + jnp.exp(-x))


def _silu(x):
    return x * _sigmoid(x)


def _softplus(x):
    return jnp.maximum(x, 0.0) + jnp.log1p(jnp.exp(-jnp.abs(x)))


def _split_bf16(x, n):
    parts = []
    r = x
    for _ in range(n):
        p = r.astype(BF16)
        parts.append(p)
        r = r - p.astype(F32)
    return parts


def _dot_split_rhs(w_bf16, x, n):
    acc = None
    for p in _split_bf16(x, n):
        t = _dot(w_bf16, p)
        acc = t if acc is None else acc + t
    return acc


def _modulated_norm(x_ref, mod_ref, g_ref, h_ref, base, tile):
    g = g_ref[...]
    shift = mod_ref[0:1, :]
    scale1 = 1.0 + mod_ref[1:2, :]
    for r0 in range(0, tile, ROWS):
        x = x_ref[pl.ds(base + r0, ROWS), :]
        ms = jnp.mean(x * x, axis=-1, keepdims=True)
        y = x * lax.rsqrt(ms + EPS) * g
        h_ref[r0:r0 + ROWS, :] = (y * scale1 + shift).astype(BF16)


def _store_slabs(ext_ref, row0, value):
    rows = value.shape[0]
    for s in range(ext_ref.shape[0]):
        ext_ref[s, row0:row0 + rows, :] = value[:, s * LANES:(s + 1) * LANES]


def _causal_conv(ext_ref, w_ref, halo, taps, r0, rows, slabs, cols):
    outs = []
    for s, c in zip(slabs, cols):
        acc = None
        for k in range(taps):
            start = halo - (taps - 1) + k + r0
            term = w_ref[k:k + 1, c:c + LANES] * ext_ref[s, start:start + rows, :]
            acc = term if acc is None else acc + term
        outs.append(acc)
    return outs[0] if len(outs) == 1 else jnp.concatenate(outs, axis=1)


def _causal_conv_long(ext_ref, w_ref, halo, taps, r0, rows, slab, col, after):
    nout = rows // SUBLANES
    w = [jnp.broadcast_to(w_ref[k:k + 1, col:col + LANES], (SUBLANES, LANES))
         for k in range(taps)]
    acc = [None] * nout
    first = halo - (taps - 1) + r0
    for o in range(rows - SUBLANES + taps):
        win = ext_ref[slab, first + o:first + o + SUBLANES, :]
        for j in range(nout):
            k = o - SUBLANES * j
            if 0 <= k < taps:
                term = w[k] * win
                if acc[j] is None:
                    acc[j] = term if after is None else term + after * 0.0
                else:
                    acc[j] = acc[j] + term
    return jnp.concatenate(acc, axis=0), acc[-1]


def _slab_cols(n):
    return [s * LANES for s in range(n)]


def _roll_history(ext_ref, halo, tile):
    ext_ref[:, 0:halo, :] = ext_ref[:, tile:tile + halo, :]


def _zero_history(ext_ref, halo):
    ext_ref[:, 0:halo, :] = jnp.zeros((ext_ref.shape[0], halo, LANES), F32)


def _adaln_kernel(c_ref, w_ref, b_ref, o_ref):
    a = _silu(c_ref[...])
    w = w_ref[...]
    a_parts = _split_bf16(a, 3)
    w_parts = _split_bf16(w, 3)
    acc = b_ref[...] + jnp.zeros(o_ref.shape, F32)
    for i in range(3):
        for j in range(3 - i):
            acc = acc + _dot(a_parts[i], w_parts[j])
    o_ref[...] = acc


def _adaln(c, w_all, b_all):
    k, d, n = w_all.shape
    bsz = c.shape[0]
    nb = D_MODEL
    return pl.pallas_call(
        _adaln_kernel,
        grid=(k, n // nb),
        in_specs=[
            pl.BlockSpec((bsz, d), lambda i, j: (0, 0)),
            pl.BlockSpec((None, d, nb), lambda i, j: (i, 0, j)),
            pl.BlockSpec((None, 1, nb), lambda i, j: (i, 0, j)),
        ],
        out_specs=pl.BlockSpec((None, bsz, nb), lambda i, j: (i, 0, j)),
        out_shape=jax.ShapeDtypeStruct((k, bsz, n), F32),
        compiler_params=pltpu.CompilerParams(
            dimension_semantics=("arbitrary", "arbitrary"), vmem_limit_bytes=VMEM_LIMIT),
        name="adaln",
    )(c, w_all, b_all.reshape(k, 1, n))


def _ssd_chunk(r0, xbc_ref, dt_ref, z_ref, gn_ref, state_ref, dtb_ref, alog_ref, dx_ref,
               gng_ref, ltri_ref, e_ref):
    q = SSD_CHUNK
    rows = slice(r0, r0 + q)
    lane = lax.broadcasted_iota(jnp.int32, (q, LANES), 1)
    row = lax.broadcasted_iota(jnp.int32, (q, LANES), 0)
    causal = row >= lane

    dt = _softplus(dt_ref[rows, :] + dtb_ref[...])
    a = -jnp.exp(alog_ref[...])
    a_cum = _dot_split_rhs(ltri_ref[...], dt * a, 3)
    a_cum_t = a_cum.T
    dt_t = dt.T
    a_last = a_cum[q - 1:q, :]

    e2 = e_ref[...]
    decay_out_x = _dot(jnp.concatenate(_split_bf16(jnp.exp(a_cum), 2), axis=1), e2)
    decay_state_x = _dot(
        jnp.concatenate(_split_bf16(jnp.exp(a_last - a_cum) * dt, 2), axis=1), e2)

    xs = xbc_ref[rows, 0:SSD_INNER]
    bm = xbc_ref[rows, SSD_INNER:SSD_INNER + LANES]
    cm = xbc_ref[rows, SSD_INNER + LANES:SSD_XBC]
    bm_b = bm.astype(BF16)
    cm_b = cm.astype(BF16)

    cb = []
    for g in range(SSD_GROUPS):
        sl = slice(g * SSD_STATE, (g + 1) * SSD_STATE)
        cb.append(lax.dot_general(cm_b[:, sl], bm_b[:, sl], (((1,), (1,)), ((), ())),
                                  preferred_element_type=F32))

    heads_per_group = SSD_HEADS // SSD_GROUPS
    y_diag = []
    for pair in range(SSD_HEADS // 2):
        wts = []
        for hh in (2 * pair, 2 * pair + 1):
            seg = a_cum[:, hh:hh + 1] - a_cum_t[hh:hh + 1, :]
            decay = jnp.exp(jnp.where(causal, seg, -jnp.inf))
            wts.append((cb[hh // heads_per_group] * decay * dt_t[hh:hh + 1, :]).astype(BF16))
        x_pair = xs[:, pair * LANES:(pair + 1) * LANES]
        x_lo = jnp.where(lane < SSD_HEAD_DIM, x_pair, 0.0).astype(BF16)
        x_hi = jnp.where(lane >= SSD_HEAD_DIM, x_pair, 0.0).astype(BF16)
        y_diag.append(_dot(jnp.concatenate(wts, axis=1), jnp.concatenate([x_lo, x_hi], axis=0)))
    y_diag = jnp.concatenate(y_diag, axis=1)

    state = state_ref[...]
    y_off = _dot(cm_b, state.astype(BF16)) * decay_out_x

    xd = (xs * decay_state_x).astype(BF16)
    new = _dot(bm.T.astype(BF16), xd)
    srow = lax.broadcasted_iota(jnp.int32, (LANES, SSD_INNER), 0)
    scol = lax.broadcasted_iota(jnp.int32, (LANES, SSD_INNER), 1)
    same_group = (srow >= SSD_STATE) == (scol >= SSD_INNER // SSD_GROUPS)
    state_ref[...] = state * decay_out_x[q - 1:q, :] + jnp.where(same_group, new, 0.0)

    y = y_diag + y_off + xs * dx_ref[...]

    z = z_ref[rows, :]
    v = y * _silu(z)
    gw = SSD_INNER // SSD_GROUPS
    for g in range(SSD_GROUPS):
        vg = v[:, g * gw:(g + 1) * gw]
        ms = jnp.mean(vg * vg, axis=-1, keepdims=True)
        gn_ref[rows, g * gw:(g + 1) * gw] = (
            vg * lax.rsqrt(ms + EPS) * gng_ref[:, g * gw:(g + 1) * gw]).astype(BF16)


def _mixer_kernel(x_ref, mod_ref, ng_ref, wina_ref, winb_ref, bg_ref, cws_ref, cbs_ref,
                  dtb_ref, alog_ref, dx_ref, gng_ref, wso_ref, cwc_ref, cbc_ref, lng_ref,
                  lnb_ref, wco_ref, cwsc_ref, wsco_ref, wo_ref, ltri_ref, e_ref,
                  o_ref,
                  h_ref, z_ref, xbcx_ref, xbc_ref, dt_ref, gn_ref, state_ref,
                  cs_ref, confx_ref, scx_ref, act_ref, gate_ref, m_ref):
    tile = SUB_TILE
    d = D_MODEL
    cw = CONF_WIDTH
    conf_slabs = range(CONF_WIDTH // LANES)
    sc_slabs = range(SC_WIDTH // LANES)
    xbc_slabs = range(SSD_XBC // LANES)

    @pl.when(pl.program_id(1) == 0)
    def _():
        _zero_history(xbcx_ref, SMALL_HALO)
        _zero_history(confx_ref, CONF_HALO)
        _zero_history(scx_ref, SMALL_HALO)
        state_ref[...] = jnp.zeros(state_ref.shape, F32)

    def sub_tile(i, carry):
        base = pl.multiple_of(i * tile, tile)
        _modulated_norm(x_ref, mod_ref, ng_ref, h_ref, base, tile)


        cs_ref[...] = _dot(h_ref[...], winb_ref[:, COL_CONF:COL_GATE])

        z_ref[...] = _dot(h_ref[...], wina_ref[:, COL_Z:COL_XBC])
        _store_slabs(xbcx_ref, SMALL_HALO, _dot(h_ref[...], wina_ref[:, COL_XBC:COL_DT]))
        dt_ref[...] = _dot(h_ref[...], wina_ref[:, COL_DT:N_IN_A])
        for r0 in range(0, tile, ROWS):
            rows = slice(r0, r0 + ROWS)
            _store_slabs(confx_ref, CONF_HALO + r0,
                         cs_ref[rows, 0:cw] * _sigmoid(cs_ref[rows, cw:2 * cw]))
            _store_slabs(scx_ref, SMALL_HALO + r0,
                         cs_ref[rows, 3 * cw:4 * cw] * cs_ref[rows, 4 * cw:5 * cw])

        gate_ref[...] = _dot(h_ref[...], winb_ref[:, COL_GATE:N_IN_B])
        after = None
        for r0 in range(0, tile, ROWS):
            rows = slice(r0, r0 + ROWS)
            outs = []
            for s_ in conf_slabs:
                o_, after = _causal_conv_long(confx_ref, cwc_ref, CONF_HALO, CONF_KERNEL, r0,
                                              ROWS, s_, s_ * LANES, after)
                outs.append(o_)
            u = jnp.concatenate(outs, axis=1) + cbc_ref[...]
            mu = jnp.mean(u, axis=-1, keepdims=True)
            uc = u - mu
            var = jnp.mean(uc * uc, axis=-1, keepdims=True)
            u = uc * lax.rsqrt(var + EPS) * lng_ref[...] + lnb_ref[...]
            act_ref[rows, 0:cw] = _silu(u).astype(BF16)
            s = _causal_conv(scx_ref, cwsc_ref, SMALL_HALO, SC_KERNEL, r0, ROWS,
                             sc_slabs, _slab_cols(len(sc_slabs)))
            act_ref[rows, cw:2 * cw] = (cs_ref[rows, 2 * cw:3 * cw] * s).astype(BF16)
        _roll_history(confx_ref, CONF_HALO, tile)
        _roll_history(scx_ref, SMALL_HALO, tile)

        m_ref[...] = (_sigmoid(gate_ref[:, d:2 * d] + bg_ref[:, d:2 * d])
                      * _dot(act_ref[:, 0:cw], wco_ref[...]))
        m_ref[...] += (_sigmoid(gate_ref[:, 2 * d:3 * d] + bg_ref[:, 2 * d:3 * d])
                       * _dot(act_ref[:, cw:2 * cw], wsco_ref[...]))
        for r0 in range(0, tile, ROWS):
            conv = _causal_conv(xbcx_ref, cws_ref, SMALL_HALO, SSD_CONV, r0, ROWS,
                                xbc_slabs, _slab_cols(len(xbc_slabs)))
            xbc_ref[r0:r0 + ROWS, :] = _silu(conv + cbs_ref[...])
        _roll_history(xbcx_ref, SMALL_HALO, tile)

        for r0 in range(0, tile, SSD_CHUNK):
            _ssd_chunk(r0, xbc_ref, dt_ref, z_ref, gn_ref, state_ref, dtb_ref, alog_ref,
                       dx_ref, gng_ref, ltri_ref, e_ref)
        m_ref[...] += (_sigmoid(gate_ref[:, 0:d] + bg_ref[:, 0:d])
                       * _dot(gn_ref[...], wso_ref[...]))

        mix = _dot(m_ref[...].astype(BF16), wo_ref[...])
        rows = pl.ds(base, tile)
        o_ref[rows, :] = x_ref[rows, :] + mod_ref[2:3, :] * mix
        return carry

    lax.fori_loop(0, x_ref.shape[0] // tile, sub_tile, 0)


def _const_spec(shape):
    zeros = (0,) * len(shape)
    return pl.BlockSpec(shape, lambda b, j: zeros, pipeline_mode=pl.Buffered(1))


def _mixer(x, mod, p, seq_tile):
    bsz, seq, d = x.shape
    tile = SUB_TILE
    consts = [p["ng"], p["w_in_a"], p["w_in_b"], p["bg"], p["cws"], p["cbs"], p["dtb"],
              p["alog"], p["dx"], p["gng"], p["wso"], p["cwc"], p["cbc"], p["lng"], p["lnb"],
              p["wco"], p["cwsc"], p["wsco"], p["wo"], p["ltri"], p["e"]]
    tok_spec = pl.BlockSpec((None, seq_tile, d), lambda b, j: (b, j, 0))
    return pl.pallas_call(
        _mixer_kernel,
        grid=(bsz, seq // seq_tile),
        in_specs=[tok_spec, pl.BlockSpec((None, 3, d), lambda b, j: (b, 0, 0))]
        + [_const_spec(a.shape) for a in consts],
        out_specs=tok_spec,
        out_shape=jax.ShapeDtypeStruct(x.shape, F32),
        scratch_shapes=[
            pltpu.VMEM((tile, d), BF16),
            pltpu.VMEM((tile, SSD_INNER), F32),
            pltpu.VMEM((SSD_XBC // LANES, SMALL_HALO + tile, LANES), F32),
            pltpu.VMEM((tile, SSD_XBC), F32),
            pltpu.VMEM((tile, LANES), F32),
            pltpu.VMEM((tile, SSD_INNER), BF16),
            pltpu.VMEM((LANES, SSD_INNER), F32),
            pltpu.VMEM((tile, COL_GATE - COL_CONF), F32),
            pltpu.VMEM((CONF_WIDTH // LANES, CONF_HALO + tile, LANES), F32),
            pltpu.VMEM((SC_WIDTH // LANES, SMALL_HALO + tile, LANES), F32),
            pltpu.VMEM((tile, 2 * CONF_WIDTH), BF16),
            pltpu.VMEM((tile, N_BRANCH * D_MODEL), F32),
            pltpu.VMEM((tile, D_MODEL), F32),
        ],
        compiler_params=pltpu.CompilerParams(
            dimension_semantics=("arbitrary", "arbitrary"), vmem_limit_bytes=VMEM_LIMIT),
        name="mixer",
    )(x, mod, *consts)


def _ffn_kernel(x_ref, mod_ref, ng_ref, wup_ref, cw_ref, cb_ref, wdn_ref, fg_ref,
                o_ref,
                h_ref, halo_ref, ext_ref, act_ref, acc_ref, *, final_norm):
    tile = SUB_TILE
    fb = FFN_BLOCK
    n_blocks = D_FF // fb
    half = fb // LANES

    @pl.when(pl.program_id(1) == 0)
    def _():
        halo_ref[...] = jnp.zeros(halo_ref.shape, F32)

    def gate_value(ref, nb, row=slice(None)):
        g0, v0 = nb * fb, D_FF + nb * fb
        return jnp.concatenate([ref[row, g0:g0 + fb], ref[row, v0:v0 + fb]], axis=1)

    def up(nb):
        ext = ext_ref.at[nb % 2]
        _store_slabs(ext, 0, halo_ref[nb])
        u = _dot(h_ref[...], gate_value(wup_ref, nb))
        _store_slabs(ext, SMALL_HALO, u)
        halo_ref[nb] = u[tile - SMALL_HALO:tile, :]

    def activate(nb):
        ext = ext_ref.at[nb % 2]
        g0, v0 = nb * fb, D_FF + nb * fb
        for r0 in range(0, tile, ROWS):
            for s in range(half):
                g = _causal_conv(ext, cw_ref, SMALL_HALO, FFN_KERNEL, r0, ROWS,
                                 [s], [g0 + s * LANES])
                v = _causal_conv(ext, cw_ref, SMALL_HALO, FFN_KERNEL, r0, ROWS,
                                 [half + s], [v0 + s * LANES])
                g = g + cb_ref[:, g0 + s * LANES:g0 + (s + 1) * LANES]
                v = v + cb_ref[:, v0 + s * LANES:v0 + (s + 1) * LANES]
                act_ref[nb % 2, r0:r0 + ROWS, s * LANES:(s + 1) * LANES] = (
                    _silu(g) * v).astype(BF16)

    def down(nb):
        part = _dot(act_ref[nb % 2], wdn_ref[nb * fb:(nb + 1) * fb, :])
        if nb == 0:
            acc_ref[...] = part
        else:
            acc_ref[...] += part

    def sub_tile(i, carry):
        base = pl.multiple_of(i * tile, tile)
        _modulated_norm(x_ref, mod_ref, ng_ref, h_ref, base, tile)

        up(0)
        for nb in range(n_blocks):
            if nb + 1 < n_blocks:
                up(nb + 1)
            if nb >= 1:
                down(nb - 1)
            activate(nb)
        down(n_blocks - 1)

        gate = mod_ref[2:3, :]
        for r0 in range(0, tile, ROWS):
            rows = pl.ds(base + r0, ROWS)
            y = x_ref[rows, :] + gate * acc_ref[r0:r0 + ROWS, :]
            if final_norm:
                ms = jnp.mean(y * y, axis=-1, keepdims=True)
                y = y * lax.rsqrt(ms + EPS) * fg_ref[...]
            o_ref[rows, :] = y
        return carry

    lax.fori_loop(0, x_ref.shape[0] // tile, sub_tile, 0)


def _ffn(x, mod, p, final_g, seq_tile, final_norm):
    bsz, seq, d = x.shape
    tile = SUB_TILE
    consts = [p["fng"], p["w_up"], p["fcw"], p["fcb"], p["w_down"], final_g]
    tok_spec = pl.BlockSpec((None, seq_tile, d), lambda b, j: (b, j, 0))
    return pl.pallas_call(
        functools.partial(_ffn_kernel, final_norm=final_norm),
        grid=(bsz, seq // seq_tile),
        in_specs=[tok_spec, pl.BlockSpec((None, 3, d), lambda b, j: (b, 0, 0))]
        + [_const_spec(a.shape) for a in consts],
        out_specs=tok_spec,
        out_shape=jax.ShapeDtypeStruct(x.shape, F32),
        scratch_shapes=[
            pltpu.VMEM((tile, d), BF16),
            pltpu.VMEM((D_FF // FFN_BLOCK, SMALL_HALO, 2 * FFN_BLOCK), F32),
            pltpu.VMEM((2, 2 * FFN_BLOCK // LANES, SMALL_HALO + tile, LANES), F32),
            pltpu.VMEM((2, tile, FFN_BLOCK), BF16),
            pltpu.VMEM((tile, d), F32),
        ],
        compiler_params=pltpu.CompilerParams(
            dimension_semantics=("arbitrary", "arbitrary"), vmem_limit_bytes=VMEM_LIMIT),
        name="ffn",
    )(x, mod, *consts)


def _pad_lanes(a, width):
    return jnp.pad(a, [(0, 0)] * (a.ndim - 1) + [(0, width - a.shape[-1])])


def _ssd_constants():
    q = SSD_CHUNK
    ltri = np.tril(np.ones((q, q), np.float32))
    e = np.zeros((LANES, SSD_INNER), np.float32)
    for hh in range(SSD_HEADS):
        e[hh, hh * SSD_HEAD_DIM:(hh + 1) * SSD_HEAD_DIM] = 1.0
    return jnp.asarray(ltri, BF16), jnp.asarray(np.concatenate([e, e], axis=0), BF16)


def _layer_params(i, w_in, b_gate, ssd_conv_w, ssd_conv_b, ssd_dt_bias, ssd_a_log, ssd_d,
                  ssd_norm_g, w_ssd_out, conf_conv_w, conf_conv_b, conf_ln_g, conf_ln_b,
                  w_conf_out, sc_conv_w, w_sc_out, w_o, norm_mix_g, norm_ffn_g, w_up,
                  ffn_conv_w, ffn_conv_b, w_down):
    n_a = SSD_INNER + SSD_XBC + SSD_HEADS
    w = w_in[i]
    ltri, e = _ssd_constants()
    row = lambda a: a.reshape(1, -1)
    return {
        "ng": row(norm_mix_g[i]),
        "w_in_a": _pad_lanes(w[:, :n_a], N_IN_A).astype(BF16),
        "w_in_b": w[:, n_a:].astype(BF16),
        "bg": row(b_gate[i]),
        "cws": ssd_conv_w[i],
        "cbs": row(ssd_conv_b[i]),
        "dtb": _pad_lanes(row(ssd_dt_bias[i]), LANES),
        "alog": _pad_lanes(row(ssd_a_log[i]), LANES),
        "dx": row(jnp.repeat(ssd_d[i], SSD_HEAD_DIM)),
        "gng": row(ssd_norm_g[i]),
        "wso": w_ssd_out[i].astype(BF16),
        "cwc": conf_conv_w[i],
        "cbc": row(conf_conv_b[i]),
        "lng": row(conf_ln_g[i]),
        "lnb": row(conf_ln_b[i]),
        "wco": w_conf_out[i].astype(BF16),
        "cwsc": sc_conv_w[i],
        "wsco": w_sc_out[i].astype(BF16),
        "wo": w_o[i].astype(BF16),
        "ltri": ltri,
        "e": e,
        "fng": row(norm_ffn_g[i]),
        "w_up": w_up[i].astype(BF16),
        "fcw": ffn_conv_w[i],
        "fcb": row(ffn_conv_b[i]),
        "w_down": w_down[i].astype(BF16),
    }


def kernel(x, c, ada_mix_w, ada_mix_b, norm_mix_g, w_in, b_gate, ssd_conv_w, ssd_conv_b, ssd_dt_bias, ssd_a_log, ssd_d, ssd_norm_g, w_ssd_out, conf_conv_w, conf_conv_b, conf_ln_g, conf_ln_b, w_conf_out, sc_conv_w, w_sc_out, w_o, ada_ffn_w, ada_ffn_b, norm_ffn_g, w_up, ffn_conv_w, ffn_conv_b, w_down, final_norm_g):
    bsz, seq, d = x.shape
    depth = w_in.shape[0]
    seq_tile = min(SEQ_TILE, seq)
    assert seq % seq_tile == 0 and seq_tile % SUB_TILE == 0 and d == D_MODEL
    assert SUB_TILE % SSD_CHUNK == 0

    mods_mix = _adaln(c, ada_mix_w, ada_mix_b).reshape(depth, bsz, 3, d)
    mods_ffn = _adaln(c, ada_ffn_w, ada_ffn_b).reshape(depth, bsz, 3, d)

    final_g = final_norm_g.reshape(1, d)
    for i in range(depth):
        p = _layer_params(i, w_in, b_gate, ssd_conv_w, ssd_conv_b, ssd_dt_bias, ssd_a_log,
                          ssd_d, ssd_norm_g, w_ssd_out, conf_conv_w, conf_conv_b, conf_ln_g,
                          conf_ln_b, w_conf_out, sc_conv_w, w_sc_out, w_o, norm_mix_g,
                          norm_ffn_g, w_up, ffn_conv_w, ffn_conv_b, w_down)
        x = _mixer(x, mods_mix[i], p, seq_tile)
        x = _ffn(x, mods_ffn[i], p, final_g, seq_tile, final_norm=(i == depth - 1))
    return x
```

```python
import functools

import numpy as np
import jax
import jax.numpy as jnp
from jax import lax
from jax.experimental import pallas as pl
from jax.experimental.pallas import tpu as pltpu

D_MODEL = 1024
SSD_HEADS = 16
SSD_HEAD_DIM = 64
SSD_INNER = SSD_HEADS * SSD_HEAD_DIM
SSD_GROUPS = 2
SSD_STATE = 64
SSD_CONV = 4
SSD_CHUNK = 128
SSD_XBC = SSD_INNER + 2 * SSD_GROUPS * SSD_STATE
CONF_WIDTH = 512
CONF_KERNEL = 31
SC_WIDTH = 512
SC_KERNEL = 3
N_BRANCH = 3
D_FF = 2816
FFN_KERNEL = 3
EPS = 1e-6

LANES = 128
SUBLANES = 8
VMEM_LIMIT = 58 * 1024 * 1024

COL_Z = 0
COL_XBC = COL_Z + SSD_INNER
COL_DT = COL_XBC + SSD_XBC
N_IN_A = COL_DT + LANES
COL_CONF = 0
COL_SC = COL_CONF + 2 * CONF_WIDTH
COL_GATE = COL_SC + 3 * SC_WIDTH
N_IN_B = COL_GATE + N_BRANCH * D_MODEL

SEQ_TILE = 1024
SUB_TILE = 256
ROWS = 32
FFN_BLOCK = 256
CONF_HALO = 32
SMALL_HALO = SUBLANES

F32 = jnp.float32
BF16 = jnp.bfloat16

_dot = functools.partial(jnp.dot, preferred_element_type=F32)


def _sigmoid(x):
    return 1.0 / (1.0 + jnp.exp(-x))


def _silu(x):
    return x * _sigmoid(x)


def _softplus(x):
    return jnp.maximum(x, 0.0) + jnp.log1p(jnp.exp(-jnp.abs(x)))


def _split_bf16(x, n):
    parts = []
    r = x
    for _ in range(n):
        p = r.astype(BF16)
        parts.append(p)
        r = r - p.astype(F32)
    return parts


def _dot_split_rhs(w_bf16, x, n):
    acc = None
    for p in _split_bf16(x, n):
        t = _dot(w_bf16, p)
        acc = t if acc is None else acc + t
    return acc


def _modulated_norm(x_ref, mod_ref, g_ref, h_ref, base, tile):
    g = g_ref[...]
    shift = mod_ref[0:1, :]
    scale1 = 1.0 + mod_ref[1:2, :]
    for r0 in range(0, tile, ROWS):
        x = x_ref[pl.ds(base + r0, ROWS), :]
        ms = jnp.mean(x * x, axis=-1, keepdims=True)
        y = x * lax.rsqrt(ms + EPS) * g
        h_ref[r0:r0 + ROWS, :] = (y * scale1 + shift).astype(BF16)


def _store_slabs(ext_ref, row0, value):
    rows = value.shape[0]
    for s in range(ext_ref.shape[0]):
        ext_ref[s, row0:row0 + rows, :] = value[:, s * LANES:(s + 1) * LANES]


def _causal_conv(ext_ref, w_ref, halo, taps, r0, rows, slabs, cols):
    outs = []
    for s, c in zip(slabs, cols):
        acc = None
        for k in range(taps):
            start = halo - (taps - 1) + k + r0
            term = w_ref[k:k + 1, c:c + LANES] * ext_ref[s, start:start + rows, :]
            acc = term if acc is None else acc + term
        outs.append(acc)
    return outs[0] if len(outs) == 1 else jnp.concatenate(outs, axis=1)


def _causal_conv_long(ext_ref, w_ref, halo, taps, r0, rows, slab, col, after):
    nout = rows // SUBLANES
    w = [jnp.broadcast_to(w_ref[k:k + 1, col:col + LANES], (SUBLANES, LANES))
         for k in range(taps)]
    acc = [None] * nout
    first = halo - (taps - 1) + r0
    for o in range(rows - SUBLANES + taps):
        win = ext_ref[slab, first + o:first + o + SUBLANES, :]
        for j in range(nout):
            k = o - SUBLANES * j
            if 0 <= k < taps:
                term = w[k] * win
                if acc[j] is None:
                    acc[j] = term if after is None else term + after * 0.0
                else:
                    acc[j] = acc[j] + term
    return jnp.concatenate(acc, axis=0), acc[-1]


def _slab_cols(n):
    return [s * LANES for s in range(n)]


def _roll_history(ext_ref, halo, tile):
    ext_ref[:, 0:halo, :] = ext_ref[:, tile:tile + halo, :]


def _zero_history(ext_ref, halo):
    ext_ref[:, 0:halo, :] = jnp.zeros((ext_ref.shape[0], halo, LANES), F32)


def _adaln_kernel(c_ref, w_ref, b_ref, o_ref):
    a = _silu(c_ref[...])
    w = w_ref[...]
    a_parts = _split_bf16(a, 2)
    w_parts = _split_bf16(w, 2)
    acc = b_ref[...] + jnp.zeros(o_ref.shape, F32)
    for i in range(2):
        for j in range(2 - i):
            acc = acc + _dot(a_parts[i], w_parts[j])
    o_ref[...] = acc


def _adaln(c, w_all, b_all):
    k, d, n = w_all.shape
    bsz = c.shape[0]
    nb = D_MODEL
    return pl.pallas_call(
        _adaln_kernel,
        grid=(k, n // nb),
        in_specs=[
            pl.BlockSpec((bsz, d), lambda i, j: (0, 0)),
            pl.BlockSpec((None, d, nb), lambda i, j: (i, 0, j)),
            pl.BlockSpec((None, 1, nb), lambda i, j: (i, 0, j)),
        ],
        out_specs=pl.BlockSpec((None, bsz, nb), lambda i, j: (i, 0, j)),
        out_shape=jax.ShapeDtypeStruct((k, bsz, n), F32),
        compiler_params=pltpu.CompilerParams(
            dimension_semantics=("arbitrary", "arbitrary"), vmem_limit_bytes=VMEM_LIMIT),
        name="adaln",
    )(c, w_all, b_all.reshape(k, 1, n))


def _ssd_chunk(r0, xbc_ref, dt_ref, z_ref, gn_ref, state_ref, dtb_ref, alog_ref, dx_ref,
               gng_ref, ltri_ref, e_ref):
    q = SSD_CHUNK
    rows = slice(r0, r0 + q)
    lane = lax.broadcasted_iota(jnp.int32, (q, LANES), 1)
    row = lax.broadcasted_iota(jnp.int32, (q, LANES), 0)
    causal = row >= lane

    dt = _softplus(dt_ref[rows, :] + dtb_ref[...])
    a = -jnp.exp(alog_ref[...])
    a_cum = _dot_split_rhs(ltri_ref[...], dt * a, 3)
    a_cum_t = a_cum.T
    dt_t = dt.T
    a_last = a_cum[q - 1:q, :]

    e2 = e_ref[...]
    decay_out_x = _dot(jnp.concatenate(_split_bf16(jnp.exp(a_cum), 2), axis=1), e2)
    decay_state_x = _dot(
        jnp.concatenate(_split_bf16(jnp.exp(a_last - a_cum) * dt, 2), axis=1), e2)

    xs = xbc_ref[rows, 0:SSD_INNER]
    bm = xbc_ref[rows, SSD_INNER:SSD_INNER + LANES]
    cm = xbc_ref[rows, SSD_INNER + LANES:SSD_XBC]
    bm_b = bm.astype(BF16)
    cm_b = cm.astype(BF16)

    cb = []
    for g in range(SSD_GROUPS):
        sl = slice(g * SSD_STATE, (g + 1) * SSD_STATE)
        cb.append(lax.dot_general(cm_b[:, sl], bm_b[:, sl], (((1,), (1,)), ((), ())),
                                  preferred_element_type=F32))

    heads_per_group = SSD_HEADS // SSD_GROUPS
    y_diag = []
    for pair in range(SSD_HEADS // 2):
        wts = []
        for hh in (2 * pair, 2 * pair + 1):
            seg = a_cum[:, hh:hh + 1] - a_cum_t[hh:hh + 1, :]
            decay = jnp.exp(jnp.where(causal, seg, -jnp.inf))
            wts.append((cb[hh // heads_per_group] * decay * dt_t[hh:hh + 1, :]).astype(BF16))
        x_pair = xs[:, pair * LANES:(pair + 1) * LANES]
        x_lo = jnp.where(lane < SSD_HEAD_DIM, x_pair, 0.0).astype(BF16)
        x_hi = jnp.where(lane >= SSD_HEAD_DIM, x_pair, 0.0).astype(BF16)
        y_diag.append(_dot(jnp.concatenate(wts, axis=1), jnp.concatenate([x_lo, x_hi], axis=0)))
    y_diag = jnp.concatenate(y_diag, axis=1)

    state = state_ref[...]
    y_off = _dot(cm_b, state.astype(BF16)) * decay_out_x

    xd = (xs * decay_state_x).astype(BF16)
    new = _dot(bm.T.astype(BF16), xd)
    srow = lax.broadcasted_iota(jnp.int32, (LANES, SSD_INNER), 0)
    scol = lax.broadcasted_iota(jnp.int32, (LANES, SSD_INNER), 1)
    same_group = (srow >= SSD_STATE) == (scol >= SSD_INNER // SSD_GROUPS)
    state_ref[...] = state * decay_out_x[q - 1:q, :] + jnp.where(same_group, new, 0.0)

    y = y_diag + y_off + xs * dx_ref[...]

    z = z_ref[rows, :]
    v = y * _silu(z)
    gw = SSD_INNER // SSD_GROUPS
    for g in range(SSD_GROUPS):
        vg = v[:, g * gw:(g + 1) * gw]
        ms = jnp.mean(vg * vg, axis=-1, keepdims=True)
        gn_ref[rows, g * gw:(g + 1) * gw] = (
            vg * lax.rsqrt(ms + EPS) * gng_ref[:, g * gw:(g + 1) * gw]).astype(BF16)


def _mixer_kernel(x_ref, mod_ref, ng_ref, wina_ref, winb_ref, bg_ref, cws_ref, cbs_ref,
                  dtb_ref, alog_ref, dx_ref, gng_ref, wso_ref, cwc_ref, cbc_ref, lng_ref,
                  lnb_ref, wco_ref, cwsc_ref, wsco_ref, wo_ref, ltri_ref, e_ref,
                  o_ref,
                  h_ref, z_ref, xbcx_ref, xbc_ref, dt_ref, gn_ref, state_ref,
                  cs_ref, confx_ref, scx_ref, act_ref, gate_ref, m_ref):
    tile = SUB_TILE
    d = D_MODEL
    cw = CONF_WIDTH
    conf_slabs = range(CONF_WIDTH // LANES)
    sc_slabs = range(SC_WIDTH // LANES)
    xbc_slabs = range(SSD_XBC // LANES)

    @pl.when(pl.program_id(1) == 0)
    def _():
        _zero_history(xbcx_ref, SMALL_HALO)
        _zero_history(confx_ref, CONF_HALO)
        _zero_history(scx_ref, SMALL_HALO)
        state_ref[...] = jnp.zeros(state_ref.shape, F32)

    def sub_tile(i, carry):
        base = pl.multiple_of(i * tile, tile)
        _modulated_norm(x_ref, mod_ref, ng_ref, h_ref, base, tile)


        cs_ref[...] = _dot(h_ref[...], winb_ref[:, COL_CONF:COL_GATE])

        z_ref[...] = _dot(h_ref[...], wina_ref[:, COL_Z:COL_XBC])
        _store_slabs(xbcx_ref, SMALL_HALO, _dot(h_ref[...], wina_ref[:, COL_XBC:COL_DT]))
        dt_ref[...] = _dot(h_ref[...], wina_ref[:, COL_DT:N_IN_A])
        for r0 in range(0, tile, ROWS):
            rows = slice(r0, r0 + ROWS)
            _store_slabs(confx_ref, CONF_HALO + r0,
                         cs_ref[rows, 0:cw] * _sigmoid(cs_ref[rows, cw:2 * cw]))
            _store_slabs(scx_ref, SMALL_HALO + r0,
                         cs_ref[rows, 3 * cw:4 * cw] * cs_ref[rows, 4 * cw:5 * cw])

        gate_ref[...] = _dot(h_ref[...], winb_ref[:, COL_GATE:N_IN_B])
        after = None
        for r0 in range(0, tile, ROWS):
            rows = slice(r0, r0 + ROWS)
            outs = []
            for s_ in conf_slabs:
                o_, after = _causal_conv_long(confx_ref, cwc_ref, CONF_HALO, CONF_KERNEL, r0,
                                              ROWS, s_, s_ * LANES, after)
                outs.append(o_)
            u = jnp.concatenate(outs, axis=1) + cbc_ref[...]
            mu = jnp.mean(u, axis=-1, keepdims=True)
            uc = u - mu
            var = jnp.mean(uc * uc, axis=-1, keepdims=True)
            u = uc * lax.rsqrt(var + EPS) * lng_ref[...] + lnb_ref[...]
            act_ref[rows, 0:cw] = _silu(u).astype(BF16)
            s = _causal_conv(scx_ref, cwsc_ref, SMALL_HALO, SC_KERNEL, r0, ROWS,
                             sc_slabs, _slab_cols(len(sc_slabs)))
            act_ref[rows, cw:2 * cw] = (cs_ref[rows, 2 * cw:3 * cw] * s).astype(BF16)
        _roll_history(confx_ref, CONF_HALO, tile)
        _roll_history(scx_ref, SMALL_HALO, tile)

        m_ref[...] = (_sigmoid(gate_ref[:, d:2 * d] + bg_ref[:, d:2 * d])
                      * _dot(act_ref[:, 0:cw], wco_ref[...]))
        m_ref[...] += (_sigmoid(gate_ref[:, 2 * d:3 * d] + bg_ref[:, 2 * d:3 * d])
                       * _dot(act_ref[:, cw:2 * cw], wsco_ref[...]))
        for r0 in range(0, tile, ROWS):
            conv = _causal_conv(xbcx_ref, cws_ref, SMALL_HALO, SSD_CONV, r0, ROWS,
                                xbc_slabs, _slab_cols(len(xbc_slabs)))
            xbc_ref[r0:r0 + ROWS, :] = _silu(conv + cbs_ref[...])
        _roll_history(xbcx_ref, SMALL_HALO, tile)

        for r0 in range(0, tile, SSD_CHUNK):
            _ssd_chunk(r0, xbc_ref, dt_ref, z_ref, gn_ref, state_ref, dtb_ref, alog_ref,
                       dx_ref, gng_ref, ltri_ref, e_ref)
        m_ref[...] += (_sigmoid(gate_ref[:, 0:d] + bg_ref[:, 0:d])
                       * _dot(gn_ref[...], wso_ref[...]))

        mix = _dot(m_ref[...].astype(BF16), wo_ref[...])
        rows = pl.ds(base, tile)
        o_ref[rows, :] = x_ref[rows, :] + mod_ref[2:3, :] * mix
        return carry

    lax.fori_loop(0, x_ref.shape[0] // tile, sub_tile, 0)


def _const_spec(a, layer):
    if layer is None:
        shape, index = a.shape, (0,) * a.ndim
    else:
        shape, index = (None,) + a.shape[1:], (layer,) + (0,) * (a.ndim - 1)
    return pl.BlockSpec(shape, lambda b, j: index, pipeline_mode=pl.Buffered(1))


def _mixer(x, mod, p, layer, seq_tile):
    bsz, seq, d = x.shape
    tile = SUB_TILE
    consts = [p["ng"], p["w_in_a"], p["w_in_b"], p["bg"], p["cws"], p["cbs"], p["dtb"],
              p["alog"], p["dx"], p["gng"], p["wso"], p["cwc"], p["cbc"], p["lng"], p["lnb"],
              p["wco"], p["cwsc"], p["wsco"], p["wo"]]
    shared = [p["ltri"], p["e"]]
    tok_spec = pl.BlockSpec((None, seq_tile, d), lambda b, j: (b, j, 0))
    return pl.pallas_call(
        _mixer_kernel,
        grid=(bsz, seq // seq_tile),
        in_specs=[tok_spec, pl.BlockSpec((None, 3, d), lambda b, j: (b, 0, 0))]
        + [_const_spec(a, layer) for a in consts] + [_const_spec(a, None) for a in shared],
        out_specs=tok_spec,
        out_shape=jax.ShapeDtypeStruct(x.shape, F32),
        scratch_shapes=[
            pltpu.VMEM((tile, d), BF16),
            pltpu.VMEM((tile, SSD_INNER), F32),
            pltpu.VMEM((SSD_XBC // LANES, SMALL_HALO + tile, LANES), F32),
            pltpu.VMEM((tile, SSD_XBC), F32),
            pltpu.VMEM((tile, LANES), F32),
            pltpu.VMEM((tile, SSD_INNER), BF16),
            pltpu.VMEM((LANES, SSD_INNER), F32),
            pltpu.VMEM((tile, COL_GATE - COL_CONF), F32),
            pltpu.VMEM((CONF_WIDTH // LANES, CONF_HALO + tile, LANES), F32),
            pltpu.VMEM((SC_WIDTH // LANES, SMALL_HALO + tile, LANES), F32),
            pltpu.VMEM((tile, 2 * CONF_WIDTH), BF16),
            pltpu.VMEM((tile, N_BRANCH * D_MODEL), F32),
            pltpu.VMEM((tile, D_MODEL), F32),
        ],
        compiler_params=pltpu.CompilerParams(
            dimension_semantics=("arbitrary", "arbitrary"), vmem_limit_bytes=VMEM_LIMIT),
        name="mixer",
    )(x, mod, *consts, *shared)


def _ffn_kernel(x_ref, mod_ref, ng_ref, wup_ref, cw_ref, cb_ref, wdn_ref, fg_ref,
                o_ref,
                h_ref, halo_ref, ext_ref, act_ref, acc_ref, *, final_norm):
    tile = SUB_TILE
    fb = FFN_BLOCK
    n_blocks = D_FF // fb
    half = fb // LANES

    @pl.when(pl.program_id(1) == 0)
    def _():
        halo_ref[...] = jnp.zeros(halo_ref.shape, F32)

    def gate_value(ref, nb, row=slice(None)):
        g0, v0 = nb * fb, D_FF + nb * fb
        return jnp.concatenate([ref[row, g0:g0 + fb], ref[row, v0:v0 + fb]], axis=1)

    def up(nb):
        ext = ext_ref.at[nb % 2]
        _store_slabs(ext, 0, halo_ref[nb])
        u = _dot(h_ref[...], gate_value(wup_ref, nb))
        _store_slabs(ext, SMALL_HALO, u)
        halo_ref[nb] = u[tile - SMALL_HALO:tile, :]

    def activate(nb):
        ext = ext_ref.at[nb % 2]
        g0, v0 = nb * fb, D_FF + nb * fb
        for r0 in range(0, tile, ROWS):
            for s in range(half):
                g = _causal_conv(ext, cw_ref, SMALL_HALO, FFN_KERNEL, r0, ROWS,
                                 [s], [g0 + s * LANES])
                v = _causal_conv(ext, cw_ref, SMALL_HALO, FFN_KERNEL, r0, ROWS,
                                 [half + s], [v0 + s * LANES])
                g = g + cb_ref[:, g0 + s * LANES:g0 + (s + 1) * LANES]
                v = v + cb_ref[:, v0 + s * LANES:v0 + (s + 1) * LANES]
                act_ref[nb % 2, r0:r0 + ROWS, s * LANES:(s + 1) * LANES] = (
                    _silu(g) * v).astype(BF16)

    def down(nb):
        part = _dot(act_ref[nb % 2], wdn_ref[nb * fb:(nb + 1) * fb, :])
        if nb == 0:
            acc_ref[...] = part
        else:
            acc_ref[...] += part

    def sub_tile(i, carry):
        base = pl.multiple_of(i * tile, tile)
        _modulated_norm(x_ref, mod_ref, ng_ref, h_ref, base, tile)

        up(0)
        for nb in range(n_blocks):
            if nb + 1 < n_blocks:
                up(nb + 1)
            if nb >= 1:
                down(nb - 1)
            activate(nb)
        down(n_blocks - 1)

        gate = mod_ref[2:3, :]
        for r0 in range(0, tile, ROWS):
            rows = pl.ds(base + r0, ROWS)
            y = x_ref[rows, :] + gate * acc_ref[r0:r0 + ROWS, :]
            if final_norm:
                ms = jnp.mean(y * y, axis=-1, keepdims=True)
                y = y * lax.rsqrt(ms + EPS) * fg_ref[...]
            o_ref[rows, :] = y
        return carry

    lax.fori_loop(0, x_ref.shape[0] // tile, sub_tile, 0)


def _ffn(x, mod, p, final_g, layer, seq_tile, final_norm):
    bsz, seq, d = x.shape
    tile = SUB_TILE
    consts = [p["fng"], p["w_up"], p["fcw"], p["fcb"], p["w_down"]]
    tok_spec = pl.BlockSpec((None, seq_tile, d), lambda b, j: (b, j, 0))
    return pl.pallas_call(
        functools.partial(_ffn_kernel, final_norm=final_norm),
        grid=(bsz, seq // seq_tile),
        in_specs=[tok_spec, pl.BlockSpec((None, 3, d), lambda b, j: (b, 0, 0))]
        + [_const_spec(a, layer) for a in consts] + [_const_spec(final_g, None)],
        out_specs=tok_spec,
        out_shape=jax.ShapeDtypeStruct(x.shape, F32),
        scratch_shapes=[
            pltpu.VMEM((tile, d), BF16),
            pltpu.VMEM((D_FF // FFN_BLOCK, SMALL_HALO, 2 * FFN_BLOCK), F32),
            pltpu.VMEM((2, 2 * FFN_BLOCK // LANES, SMALL_HALO + tile, LANES), F32),
            pltpu.VMEM((2, tile, FFN_BLOCK), BF16),
            pltpu.VMEM((tile, d), F32),
        ],
        compiler_params=pltpu.CompilerParams(
            dimension_semantics=("arbitrary", "arbitrary"), vmem_limit_bytes=VMEM_LIMIT),
        name="ffn",
    )(x, mod, *consts, final_g)


def _pad_lanes(a, width):
    return jnp.pad(a, [(0, 0)] * (a.ndim - 1) + [(0, width - a.shape[-1])])


def _ssd_constants():
    q = SSD_CHUNK
    ltri = np.tril(np.ones((q, q), np.float32))
    e = np.zeros((LANES, SSD_INNER), np.float32)
    for hh in range(SSD_HEADS):
        e[hh, hh * SSD_HEAD_DIM:(hh + 1) * SSD_HEAD_DIM] = 1.0
    return jnp.asarray(ltri, BF16), jnp.asarray(np.concatenate([e, e], axis=0), BF16)


def _params(w_in, b_gate, ssd_conv_w, ssd_conv_b, ssd_dt_bias, ssd_a_log, ssd_d, ssd_norm_g,
            w_ssd_out, conf_conv_w, conf_conv_b, conf_ln_g, conf_ln_b, w_conf_out, sc_conv_w,
            w_sc_out, w_o, norm_mix_g, norm_ffn_g, w_up, ffn_conv_w, ffn_conv_b, w_down):
    n_a = SSD_INNER + SSD_XBC + SSD_HEADS
    depth = w_in.shape[0]
    ltri, e = _ssd_constants()
    row = lambda a: a.reshape(depth, 1, -1)
    return {
        "ng": row(norm_mix_g),
        "w_in_a": _pad_lanes(w_in[:, :, :n_a], N_IN_A).astype(BF16),
        "w_in_b": w_in[:, :, n_a:].astype(BF16),
        "bg": row(b_gate),
        "cws": ssd_conv_w,
        "cbs": row(ssd_conv_b),
        "dtb": _pad_lanes(row(ssd_dt_bias), LANES),
        "alog": _pad_lanes(row(ssd_a_log), LANES),
        "dx": row(jnp.repeat(ssd_d, SSD_HEAD_DIM, axis=-1)),
        "gng": row(ssd_norm_g),
        "wso": w_ssd_out.astype(BF16),
        "cwc": conf_conv_w,
        "cbc": row(conf_conv_b),
        "lng": row(conf_ln_g),
        "lnb": row(conf_ln_b),
        "wco": w_conf_out.astype(BF16),
        "cwsc": sc_conv_w,
        "wsco": w_sc_out.astype(BF16),
        "wo": w_o.astype(BF16),
        "ltri": ltri,
        "e": e,
        "fng": row(norm_ffn_g),
        "w_up": w_up.astype(BF16),
        "fcw": ffn_conv_w,
        "fcb": row(ffn_conv_b),
        "w_down": w_down.astype(BF16),
    }


def kernel(x, c, ada_mix_w, ada_mix_b, norm_mix_g, w_in, b_gate, ssd_conv_w, ssd_conv_b, ssd_dt_bias, ssd_a_log, ssd_d, ssd_norm_g, w_ssd_out, conf_conv_w, conf_conv_b, conf_ln_g, conf_ln_b, w_conf_out, sc_conv_w, w_sc_out, w_o, ada_ffn_w, ada_ffn_b, norm_ffn_g, w_up, ffn_conv_w, ffn_conv_b, w_down, final_norm_g):
    bsz, seq, d = x.shape
    depth = w_in.shape[0]
    seq_tile = min(SEQ_TILE, seq)
    assert seq % seq_tile == 0 and seq_tile % SUB_TILE == 0 and d == D_MODEL
    assert SUB_TILE % SSD_CHUNK == 0

    mods_mix = _adaln(c, ada_mix_w, ada_mix_b).reshape(depth, bsz, 3, d)
    mods_ffn = _adaln(c, ada_ffn_w, ada_ffn_b).reshape(depth, bsz, 3, d)

    final_g = final_norm_g.reshape(1, d)
    p = _params(w_in, b_gate, ssd_conv_w, ssd_conv_b, ssd_dt_bias, ssd_a_log, ssd_d,
                ssd_norm_g, w_ssd_out, conf_conv_w, conf_conv_b, conf_ln_g, conf_ln_b,
                w_conf_out, sc_conv_w, w_sc_out, w_o, norm_mix_g, norm_ffn_g, w_up,
                ffn_conv_w, ffn_conv_b, w_down)
    for i in range(depth):
        x = _mixer(x, mods_mix[i], p, i, seq_tile)
        x = _ffn(x, mods_ffn[i], p, final_g, i, seq_tile, final_norm=(i == depth - 1))
    return x
```

```python
import functools

import numpy as np
import jax
import jax.numpy as jnp
from jax import lax
from jax.experimental import pallas as pl
from jax.experimental.pallas import tpu as pltpu

D_MODEL = 1024
SSD_HEADS = 16
SSD_HEAD_DIM = 64
SSD_INNER = SSD_HEADS * SSD_HEAD_DIM
SSD_GROUPS = 2
SSD_STATE = 64
SSD_CONV = 4
SSD_CHUNK = 128
SSD_XBC = SSD_INNER + 2 * SSD_GROUPS * SSD_STATE
CONF_WIDTH = 512
CONF_KERNEL = 31
SC_WIDTH = 512
SC_KERNEL = 3
N_BRANCH = 3
D_FF = 2816
FFN_KERNEL = 3
EPS = 1e-6

LANES = 128
SUBLANES = 8
VMEM_LIMIT = 58 * 1024 * 1024

COL_Z = 0
COL_XBC = COL_Z + SSD_INNER
COL_DT = COL_XBC + SSD_XBC
N_IN_A = COL_DT + LANES
COL_CONF = 0
COL_SC = COL_CONF + 2 * CONF_WIDTH
COL_GATE = COL_SC + 3 * SC_WIDTH
N_IN_B = COL_GATE + N_BRANCH * D_MODEL

SEQ_TILE = 1024
SUB_TILE = 256
ROWS = 32
CONF_ROWS = 16
FFN_BLOCK = 256
CONF_HALO = 32
SMALL_HALO = SUBLANES

F32 = jnp.float32
BF16 = jnp.bfloat16

_dot = functools.partial(jnp.dot, preferred_element_type=F32)


def _sigmoid(x):
    return 1.0 / (1.0 + jnp.exp(-x))


def _silu(x):
    return x * _sigmoid(x)


def _softplus(x):
    return jnp.maximum(x, 0.0) + jnp.log1p(jnp.exp(-jnp.abs(x)))


def _split_bf16(x, n):
    parts = []
    r = x
    for _ in range(n):
        p = r.astype(BF16)
        parts.append(p)
        r = r - p.astype(F32)
    return parts


def _dot_split_rhs(w_bf16, x, n):
    acc = None
    for p in _split_bf16(x, n):
        t = _dot(w_bf16, p)
        acc = t if acc is None else acc + t
    return acc


def _modulated_norm(x_ref, mod_ref, g_ref, h_ref, base, tile):
    g = g_ref[...]
    shift = mod_ref[0:1, :]
    scale1 = 1.0 + mod_ref[1:2, :]
    for r0 in range(0, tile, ROWS):
        x = x_ref[pl.ds(base + r0, ROWS), :]
        ms = jnp.mean(x * x, axis=-1, keepdims=True)
        y = x * lax.rsqrt(ms + EPS) * g
        h_ref[r0:r0 + ROWS, :] = (y * scale1 + shift).astype(BF16)


def _store_slabs(ext_ref, row0, value):
    rows = value.shape[0]
    for s in range(ext_ref.shape[0]):
        ext_ref[s, row0:row0 + rows, :] = value[:, s * LANES:(s + 1) * LANES]


def _causal_conv(ext_ref, w_ref, halo, taps, r0, rows, slabs, cols):
    outs = []
    for s, c in zip(slabs, cols):
        acc = None
        for k in range(taps):
            start = halo - (taps - 1) + k + r0
            term = w_ref[k:k + 1, c:c + LANES] * ext_ref[s, start:start + rows, :]
            acc = term if acc is None else acc + term
        outs.append(acc)
    return outs[0] if len(outs) == 1 else jnp.concatenate(outs, axis=1)


def _causal_conv_long(ext_ref, w_ref, halo, taps, r0, rows, slab, col, after):
    nout = rows // SUBLANES
    w = [jnp.broadcast_to(w_ref[k:k + 1, col:col + LANES], (SUBLANES, LANES))
         for k in range(taps)]
    acc = [None] * nout
    first = halo - (taps - 1) + r0
    for o in range(rows - SUBLANES + taps):
        win = ext_ref[slab, first + o:first + o + SUBLANES, :]
        for j in range(nout):
            k = o - SUBLANES * j
            if 0 <= k < taps:
                term = w[k] * win
                if acc[j] is None:
                    acc[j] = term if after is None else term + after * 0.0
                else:
                    acc[j] = acc[j] + term
    return jnp.concatenate(acc, axis=0), acc[-1]


def _slab_cols(n):
    return [s * LANES for s in range(n)]


def _roll_history(ext_ref, halo, tile):
    ext_ref[:, 0:halo, :] = ext_ref[:, tile:tile + halo, :]


def _zero_history(ext_ref, halo):
    ext_ref[:, 0:halo, :] = jnp.zeros((ext_ref.shape[0], halo, LANES), F32)


def _adaln_kernel(c_ref, w_ref, b_ref, o_ref):
    a = _silu(c_ref[...])
    w = w_ref[...]
    a_parts = _split_bf16(a, 2)
    w_parts = _split_bf16(w, 2)
    acc = b_ref[...] + jnp.zeros(o_ref.shape, F32)
    for i in range(2):
        for j in range(2 - i):
            acc = acc + _dot(a_parts[i], w_parts[j])
    o_ref[...] = acc


def _adaln(c, w_all, b_all):
    k, d, n = w_all.shape
    bsz = c.shape[0]
    nb = D_MODEL
    return pl.pallas_call(
        _adaln_kernel,
        grid=(k, n // nb),
        in_specs=[
            pl.BlockSpec((bsz, d), lambda i, j: (0, 0)),
            pl.BlockSpec((None, d, nb), lambda i, j: (i, 0, j)),
            pl.BlockSpec((None, 1, nb), lambda i, j: (i, 0, j)),
        ],
        out_specs=pl.BlockSpec((None, bsz, nb), lambda i, j: (i, 0, j)),
        out_shape=jax.ShapeDtypeStruct((k, bsz, n), F32),
        compiler_params=pltpu.CompilerParams(
            dimension_semantics=("arbitrary", "arbitrary"), vmem_limit_bytes=VMEM_LIMIT),
        name="adaln",
    )(c, w_all, b_all.reshape(k, 1, n))


def _ssd_chunk(r0, xbc_ref, dt_ref, z_ref, gn_ref, state_ref, dtb_ref, alog_ref, dx_ref,
               gng_ref, ltri_ref, e_ref):
    q = SSD_CHUNK
    rows = slice(r0, r0 + q)
    lane = lax.broadcasted_iota(jnp.int32, (q, LANES), 1)
    row = lax.broadcasted_iota(jnp.int32, (q, LANES), 0)
    causal = row >= lane

    dt = _softplus(dt_ref[rows, :] + dtb_ref[...])
    a = -jnp.exp(alog_ref[...])
    a_cum = _dot_split_rhs(ltri_ref[...], dt * a, 3)
    a_cum_t = a_cum.T
    dt_t = dt.T
    a_last = a_cum[q - 1:q, :]

    e2 = e_ref[...]
    decay_out_x = _dot(jnp.concatenate(_split_bf16(jnp.exp(a_cum), 2), axis=1), e2)
    decay_state_x = _dot(
        jnp.concatenate(_split_bf16(jnp.exp(a_last - a_cum) * dt, 2), axis=1), e2)

    xs = xbc_ref[rows, 0:SSD_INNER]
    bm = xbc_ref[rows, SSD_INNER:SSD_INNER + LANES]
    cm = xbc_ref[rows, SSD_INNER + LANES:SSD_XBC]
    bm_b = bm.astype(BF16)
    cm_b = cm.astype(BF16)

    cb = []
    for g in range(SSD_GROUPS):
        sl = slice(g * SSD_STATE, (g + 1) * SSD_STATE)
        cb.append(lax.dot_general(cm_b[:, sl], bm_b[:, sl], (((1,), (1,)), ((), ())),
                                  preferred_element_type=F32))

    heads_per_group = SSD_HEADS // SSD_GROUPS
    y_diag = []
    for pair in range(SSD_HEADS // 2):
        wts = []
        for hh in (2 * pair, 2 * pair + 1):
            seg = a_cum[:, hh:hh + 1] - a_cum_t[hh:hh + 1, :]
            decay = jnp.exp(jnp.where(causal, seg, -jnp.inf))
            wts.append((cb[hh // heads_per_group] * decay * dt_t[hh:hh + 1, :]).astype(BF16))
        x_pair = xs[:, pair * LANES:(pair + 1) * LANES]
        x_lo = jnp.where(lane < SSD_HEAD_DIM, x_pair, 0.0).astype(BF16)
        x_hi = jnp.where(lane >= SSD_HEAD_DIM, x_pair, 0.0).astype(BF16)
        y_diag.append(_dot(jnp.concatenate(wts, axis=1), jnp.concatenate([x_lo, x_hi], axis=0)))
    y_diag = jnp.concatenate(y_diag, axis=1)

    state = state_ref[...]
    y_off = _dot(cm_b, state.astype(BF16)) * decay_out_x

    xd = (xs * decay_state_x).astype(BF16)
    new = _dot(bm.T.astype(BF16), xd)
    srow = lax.broadcasted_iota(jnp.int32, (LANES, SSD_INNER), 0)
    scol = lax.broadcasted_iota(jnp.int32, (LANES, SSD_INNER), 1)
    same_group = (srow >= SSD_STATE) == (scol >= SSD_INNER // SSD_GROUPS)
    state_ref[...] = state * decay_out_x[q - 1:q, :] + jnp.where(same_group, new, 0.0)

    y = y_diag + y_off + xs * dx_ref[...]

    z = z_ref[rows, :]
    v = y * _silu(z)
    gw = SSD_INNER // SSD_GROUPS
    for g in range(SSD_GROUPS):
        vg = v[:, g * gw:(g + 1) * gw]
        ms = jnp.mean(vg * vg, axis=-1, keepdims=True)
        gn_ref[rows, g * gw:(g + 1) * gw] = (
            vg * lax.rsqrt(ms + EPS) * gng_ref[:, g * gw:(g + 1) * gw]).astype(BF16)


def _mixer_kernel(x_ref, mod_ref, ng_ref, wina_ref, winb_ref, bg_ref, cws_ref, cbs_ref,
                  dtb_ref, alog_ref, dx_ref, gng_ref, wso_ref, cwc_ref, cbc_ref, lng_ref,
                  lnb_ref, wco_ref, cwsc_ref, wsco_ref, wo_ref, ltri_ref, e_ref,
                  o_ref,
                  h_ref, z_ref, xbcx_ref, xbc_ref, dt_ref, gn_ref, state_ref,
                  cs_ref, confx_ref, scx_ref, act_ref, gate_ref, m_ref):
    tile = SUB_TILE
    d = D_MODEL
    cw = CONF_WIDTH
    conf_slabs = range(CONF_WIDTH // LANES)
    sc_slabs = range(SC_WIDTH // LANES)
    xbc_slabs = range(SSD_XBC // LANES)

    @pl.when(pl.program_id(1) == 0)
    def _():
        _zero_history(xbcx_ref, SMALL_HALO)
        _zero_history(confx_ref, CONF_HALO)
        _zero_history(scx_ref, SMALL_HALO)
        state_ref[...] = jnp.zeros(state_ref.shape, F32)

    def sub_tile(i, carry):
        base = pl.multiple_of(i * tile, tile)
        _modulated_norm(x_ref, mod_ref, ng_ref, h_ref, base, tile)


        cs_ref[...] = _dot(h_ref[...], winb_ref[:, COL_CONF:COL_GATE])

        z_ref[...] = _dot(h_ref[...], wina_ref[:, COL_Z:COL_XBC])
        _store_slabs(xbcx_ref, SMALL_HALO, _dot(h_ref[...], wina_ref[:, COL_XBC:COL_DT]))
        dt_ref[...] = _dot(h_ref[...], wina_ref[:, COL_DT:N_IN_A])
        for r0 in range(0, tile, ROWS):
            rows = slice(r0, r0 + ROWS)
            _store_slabs(confx_ref, CONF_HALO + r0,
                         cs_ref[rows, 0:cw] * _sigmoid(cs_ref[rows, cw:2 * cw]))
            _store_slabs(scx_ref, SMALL_HALO + r0,
                         cs_ref[rows, 3 * cw:4 * cw] * cs_ref[rows, 4 * cw:5 * cw])

        gate_ref[...] = _dot(h_ref[...], winb_ref[:, COL_GATE:N_IN_B])
        after = None
        for r0 in range(0, tile, CONF_ROWS):
            rows = slice(r0, r0 + CONF_ROWS)
            outs = []
            for s_ in conf_slabs:
                o_, after = _causal_conv_long(confx_ref, cwc_ref, CONF_HALO, CONF_KERNEL, r0,
                                              CONF_ROWS, s_, s_ * LANES, after)
                outs.append(o_)
            u = jnp.concatenate(outs, axis=1) + cbc_ref[...]
            mu = jnp.mean(u, axis=-1, keepdims=True)
            uc = u - mu
            var = jnp.mean(uc * uc, axis=-1, keepdims=True)
            u = uc * lax.rsqrt(var + EPS) * lng_ref[...] + lnb_ref[...]
            act_ref[rows, 0:cw] = _silu(u).astype(BF16)
        for r0 in range(0, tile, ROWS):
            rows = slice(r0, r0 + ROWS)
            s = _causal_conv(scx_ref, cwsc_ref, SMALL_HALO, SC_KERNEL, r0, ROWS,
                             sc_slabs, _slab_cols(len(sc_slabs)))
            act_ref[rows, cw:2 * cw] = (cs_ref[rows, 2 * cw:3 * cw] * s).astype(BF16)
        _roll_history(confx_ref, CONF_HALO, tile)
        _roll_history(scx_ref, SMALL_HALO, tile)

        m_ref[...] = (_sigmoid(gate_ref[:, d:2 * d] + bg_ref[:, d:2 * d])
                      * _dot(act_ref[:, 0:cw], wco_ref[...]))
        m_ref[...] += (_sigmoid(gate_ref[:, 2 * d:3 * d] + bg_ref[:, 2 * d:3 * d])
                       * _dot(act_ref[:, cw:2 * cw], wsco_ref[...]))
        for r0 in range(0, tile, ROWS):
            conv = _causal_conv(xbcx_ref, cws_ref, SMALL_HALO, SSD_CONV, r0, ROWS,
                                xbc_slabs, _slab_cols(len(xbc_slabs)))
            xbc_ref[r0:r0 + ROWS, :] = _silu(conv + cbs_ref[...])
        _roll_history(xbcx_ref, SMALL_HALO, tile)

        for r0 in range(0, tile, SSD_CHUNK):
            _ssd_chunk(r0, xbc_ref, dt_ref, z_ref, gn_ref, state_ref, dtb_ref, alog_ref,
                       dx_ref, gng_ref, ltri_ref, e_ref)
        m_ref[...] += (_sigmoid(gate_ref[:, 0:d] + bg_ref[:, 0:d])
                       * _dot(gn_ref[...], wso_ref[...]))

        mix = _dot(m_ref[...].astype(BF16), wo_ref[...])
        rows = pl.ds(base, tile)
        o_ref[rows, :] = x_ref[rows, :] + mod_ref[2:3, :] * mix
        return carry

    lax.fori_loop(0, x_ref.shape[0] // tile, sub_tile, 0)


def _const_spec(a, layer):
    if layer is None:
        shape, index = a.shape, (0,) * a.ndim
    else:
        shape, index = (None,) + a.shape[1:], (layer,) + (0,) * (a.ndim - 1)
    return pl.BlockSpec(shape, lambda b, j: index, pipeline_mode=pl.Buffered(1))


def _mixer(x, mod, p, layer, seq_tile):
    bsz, seq, d = x.shape
    tile = SUB_TILE
    consts = [p["ng"], p["w_in_a"], p["w_in_b"], p["bg"], p["cws"], p["cbs"], p["dtb"],
              p["alog"], p["dx"], p["gng"], p["wso"], p["cwc"], p["cbc"], p["lng"], p["lnb"],
              p["wco"], p["cwsc"], p["wsco"], p["wo"]]
    shared = [p["ltri"], p["e"]]
    tok_spec = pl.BlockSpec((None, seq_tile, d), lambda b, j: (b, j, 0))
    return pl.pallas_call(
        _mixer_kernel,
        grid=(bsz, seq // seq_tile),
        in_specs=[tok_spec, pl.BlockSpec((None, 3, d), lambda b, j: (b, 0, 0))]
        + [_const_spec(a, layer) for a in consts] + [_const_spec(a, None) for a in shared],
        out_specs=tok_spec,
        out_shape=jax.ShapeDtypeStruct(x.shape, F32),
        scratch_shapes=[
            pltpu.VMEM((tile, d), BF16),
            pltpu.VMEM((tile, SSD_INNER), F32),
            pltpu.VMEM((SSD_XBC // LANES, SMALL_HALO + tile, LANES), F32),
            pltpu.VMEM((tile, SSD_XBC), F32),
            pltpu.VMEM((tile, LANES), F32),
            pltpu.VMEM((tile, SSD_INNER), BF16),
            pltpu.VMEM((LANES, SSD_INNER), F32),
            pltpu.VMEM((tile, COL_GATE - COL_CONF), F32),
            pltpu.VMEM((CONF_WIDTH // LANES, CONF_HALO + tile, LANES), F32),
            pltpu.VMEM((SC_WIDTH // LANES, SMALL_HALO + tile, LANES), F32),
            pltpu.VMEM((tile, 2 * CONF_WIDTH), BF16),
            pltpu.VMEM((tile, N_BRANCH * D_MODEL), F32),
            pltpu.VMEM((tile, D_MODEL), F32),
        ],
        compiler_params=pltpu.CompilerParams(
            dimension_semantics=("arbitrary", "arbitrary"), vmem_limit_bytes=VMEM_LIMIT),
        name="mixer",
    )(x, mod, *consts, *shared)


def _ffn_kernel(x_ref, mod_ref, ng_ref, wup_ref, cw_ref, cb_ref, wdn_ref, fg_ref,
                o_ref,
                h_ref, halo_ref, ext_ref, act_ref, acc_ref, *, final_norm):
    tile = SUB_TILE
    fb = FFN_BLOCK
    n_blocks = D_FF // fb
    half = fb // LANES

    @pl.when(pl.program_id(1) == 0)
    def _():
        halo_ref[...] = jnp.zeros(halo_ref.shape, F32)

    def gate_value(ref, nb, row=slice(None)):
        g0, v0 = nb * fb, D_FF + nb * fb
        return jnp.concatenate([ref[row, g0:g0 + fb], ref[row, v0:v0 + fb]], axis=1)

    def up(nb):
        ext = ext_ref.at[nb % 2]
        _store_slabs(ext, 0, halo_ref[nb])
        u = _dot(h_ref[...], gate_value(wup_ref, nb))
        _store_slabs(ext, SMALL_HALO, u)
        halo_ref[nb] = u[tile - SMALL_HALO:tile, :]

    def activate(nb):
        ext = ext_ref.at[nb % 2]
        g0, v0 = nb * fb, D_FF + nb * fb
        for r0 in range(0, tile, ROWS):
            for s in range(half):
                g = _causal_conv(ext, cw_ref, SMALL_HALO, FFN_KERNEL, r0, ROWS,
                                 [s], [g0 + s * LANES])
                v = _causal_conv(ext, cw_ref, SMALL_HALO, FFN_KERNEL, r0, ROWS,
                                 [half + s], [v0 + s * LANES])
                g = g + cb_ref[:, g0 + s * LANES:g0 + (s + 1) * LANES]
                v = v + cb_ref[:, v0 + s * LANES:v0 + (s + 1) * LANES]
                act_ref[nb % 2, r0:r0 + ROWS, s * LANES:(s + 1) * LANES] = (
                    _silu(g) * v).astype(BF16)

    def down(nb):
        part = _dot(act_ref[nb % 2], wdn_ref[nb * fb:(nb + 1) * fb, :])
        if nb == 0:
            acc_ref[...] = part
        else:
            acc_ref[...] += part

    def sub_tile(i, carry):
        base = pl.multiple_of(i * tile, tile)
        _modulated_norm(x_ref, mod_ref, ng_ref, h_ref, base, tile)

        up(0)
        for nb in range(n_blocks):
            if nb + 1 < n_blocks:
                up(nb + 1)
            if nb >= 1:
                down(nb - 1)
            activate(nb)
        down(n_blocks - 1)

        gate = mod_ref[2:3, :]
        for r0 in range(0, tile, ROWS):
            rows = pl.ds(base + r0, ROWS)
            y = x_ref[rows, :] + gate * acc_ref[r0:r0 + ROWS, :]
            if final_norm:
                ms = jnp.mean(y * y, axis=-1, keepdims=True)
                y = y * lax.rsqrt(ms + EPS) * fg_ref[...]
            o_ref[rows, :] = y
        return carry

    lax.fori_loop(0, x_ref.shape[0] // tile, sub_tile, 0)


def _ffn(x, mod, p, final_g, layer, seq_tile, final_norm):
    bsz, seq, d = x.shape
    tile = SUB_TILE
    consts = [p["fng"], p["w_up"], p["fcw"], p["fcb"], p["w_down"]]
    tok_spec = pl.BlockSpec((None, seq_tile, d), lambda b, j: (b, j, 0))
    return pl.pallas_call(
        functools.partial(_ffn_kernel, final_norm=final_norm),
        grid=(bsz, seq // seq_tile),
        in_specs=[tok_spec, pl.BlockSpec((None, 3, d), lambda b, j: (b, 0, 0))]
        + [_const_spec(a, layer) for a in consts] + [_const_spec(final_g, None)],
        out_specs=tok_spec,
        out_shape=jax.ShapeDtypeStruct(x.shape, F32),
        scratch_shapes=[
            pltpu.VMEM((tile, d), BF16),
            pltpu.VMEM((D_FF // FFN_BLOCK, SMALL_HALO, 2 * FFN_BLOCK), F32),
            pltpu.VMEM((2, 2 * FFN_BLOCK // LANES, SMALL_HALO + tile, LANES), F32),
            pltpu.VMEM((2, tile, FFN_BLOCK), BF16),
            pltpu.VMEM((tile, d), F32),
        ],
        compiler_params=pltpu.CompilerParams(
            dimension_semantics=("arbitrary", "arbitrary"), vmem_limit_bytes=VMEM_LIMIT),
        name="ffn",
    )(x, mod, *consts, final_g)


def _pad_lanes(a, width):
    return jnp.pad(a, [(0, 0)] * (a.ndim - 1) + [(0, width - a.shape[-1])])


def _ssd_constants():
    q = SSD_CHUNK
    ltri = np.tril(np.ones((q, q), np.float32))
    e = np.zeros((LANES, SSD_INNER), np.float32)
    for hh in range(SSD_HEADS):
        e[hh, hh * SSD_HEAD_DIM:(hh + 1) * SSD_HEAD_DIM] = 1.0
    return jnp.asarray(ltri, BF16), jnp.asarray(np.concatenate([e, e], axis=0), BF16)


def _params(w_in, b_gate, ssd_conv_w, ssd_conv_b, ssd_dt_bias, ssd_a_log, ssd_d, ssd_norm_g,
            w_ssd_out, conf_conv_w, conf_conv_b, conf_ln_g, conf_ln_b, w_conf_out, sc_conv_w,
            w_sc_out, w_o, norm_mix_g, norm_ffn_g, w_up, ffn_conv_w, ffn_conv_b, w_down):
    n_a = SSD_INNER + SSD_XBC + SSD_HEADS
    depth = w_in.shape[0]
    w_in_bf16 = w_in.astype(BF16)
    ltri, e = _ssd_constants()
    row = lambda a: a.reshape(depth, 1, -1)
    return {
        "ng": row(norm_mix_g),
        "w_in_a": _pad_lanes(w_in_bf16[:, :, :n_a], N_IN_A),
        "w_in_b": w_in_bf16[:, :, n_a:],
        "bg": row(b_gate),
        "cws": ssd_conv_w,
        "cbs": row(ssd_conv_b),
        "dtb": _pad_lanes(row(ssd_dt_bias), LANES),
        "alog": _pad_lanes(row(ssd_a_log), LANES),
        "dx": row(jnp.repeat(ssd_d, SSD_HEAD_DIM, axis=-1)),
        "gng": row(ssd_norm_g),
        "wso": w_ssd_out.astype(BF16),
        "cwc": conf_conv_w,
        "cbc": row(conf_conv_b),
        "lng": row(conf_ln_g),
        "lnb": row(conf_ln_b),
        "wco": w_conf_out.astype(BF16),
        "cwsc": sc_conv_w,
        "wsco": w_sc_out.astype(BF16),
        "wo": w_o.astype(BF16),
        "ltri": ltri,
        "e": e,
        "fng": row(norm_ffn_g),
        "w_up": w_up.astype(BF16),
        "fcw": ffn_conv_w,
        "fcb": row(ffn_conv_b),
        "w_down": w_down.astype(BF16),
    }


def kernel(x, c, ada_mix_w, ada_mix_b, norm_mix_g, w_in, b_gate, ssd_conv_w, ssd_conv_b, ssd_dt_bias, ssd_a_log, ssd_d, ssd_norm_g, w_ssd_out, conf_conv_w, conf_conv_b, conf_ln_g, conf_ln_b, w_conf_out, sc_conv_w, w_sc_out, w_o, ada_ffn_w, ada_ffn_b, norm_ffn_g, w_up, ffn_conv_w, ffn_conv_b, w_down, final_norm_g):
    bsz, seq, d = x.shape
    depth = w_in.shape[0]
    seq_tile = min(SEQ_TILE, seq)
    assert seq % seq_tile == 0 and seq_tile % SUB_TILE == 0 and d == D_MODEL
    assert SUB_TILE % SSD_CHUNK == 0

    mods_mix = _adaln(c, ada_mix_w, ada_mix_b).reshape(depth, bsz, 3, d)
    mods_ffn = _adaln(c, ada_ffn_w, ada_ffn_b).reshape(depth, bsz, 3, d)

    final_g = final_norm_g.reshape(1, d)
    p = _params(w_in, b_gate, ssd_conv_w, ssd_conv_b, ssd_dt_bias, ssd_a_log, ssd_d,
                ssd_norm_g, w_ssd_out, conf_conv_w, conf_conv_b, conf_ln_g, conf_ln_b,
                w_conf_out, sc_conv_w, w_sc_out, w_o, norm_mix_g, norm_ffn_g, w_up,
                ffn_conv_w, ffn_conv_b, w_down)
    for i in range(depth):
        x = _mixer(x, mods_mix[i], p, i, seq_tile)
        x = _ffn(x, mods_ffn[i], p, final_g, i, seq_tile, final_norm=(i == depth - 1))
    return x
```
